```python
import math
import jax, jax.numpy as jnp
from jax import lax
import numpy as np

D_MODEL = 2048
BATCH = 4
SEQ = 2048
DEPTH = 1

MEM_LEN = 256
DIFF_HEADS = 8
DIFF_HEAD_DIM = 128
DIFF_QK_WIDTH = DIFF_HEADS * 2 * DIFF_HEAD_DIM
DIFF_V_WIDTH = DIFF_HEADS * 2 * DIFF_HEAD_DIM
Q_BLOCK = 128
CONV_WIDTH = D_MODEL
CONV_K = 3
MIX_IN_SPLITS = (DIFF_QK_WIDTH, DIFF_QK_WIDTH, DIFF_V_WIDTH,
                 CONV_WIDTH, CONV_WIDTH, CONV_WIDTH,
                 D_MODEL, D_MODEL)
MIX_IN_WIDTH = sum(MIX_IN_SPLITS)
MIX_IN_OFFSETS = tuple(int(o) for o in np.cumsum(MIX_IN_SPLITS)[:-1])
N_BRANCHES = 2
XATTN_HEADS = 4
XATTN_HEAD_DIM = 128
XATTN_WIDTH = XATTN_HEADS * XATTN_HEAD_DIM
D_FF = 128 * ((8 * D_MODEL // 3 + 127) // 128)
NORM_EPS = 1e-6
SUBLN_EPS = 1e-5

kernel_name = 'hybrid_diffattn_shortconv_macaron'


def rms_norm(x, g, eps=NORM_EPS):
    xf = x.astype(jnp.float32)
    y = xf * lax.rsqrt(jnp.mean(xf * xf, axis=-1, keepdims=True) + eps)
    return (y * g.astype(jnp.float32)).astype(x.dtype)


def swiglu(h, w_gate, w_up, w_down):
    return (jax.nn.silu(h @ w_gate) * (h @ w_up)) @ w_down


def diff_lambda(lq1, lk1, lq2, lk2, lam_init):
    f = jnp.float32
    return (jnp.exp(jnp.dot(lq1.astype(f), lk1.astype(f)))
            - jnp.exp(jnp.dot(lq2.astype(f), lk2.astype(f))) + lam_init)


def diff_attention(q, k, v, lam):
    s = q.shape[1]
    scale = q.shape[-1] ** -0.5
    qh = jnp.transpose(q, (0, 2, 3, 1, 4)) * scale
    kh = jnp.transpose(k, (0, 2, 3, 1, 4))
    vh = jnp.transpose(v, (0, 2, 1, 3))
    outs = []
    for i in range(s // Q_BLOCK):
        q0, q1 = i * Q_BLOCK, (i + 1) * Q_BLOCK
        qb = qh[:, :, :, q0:q1]
        kb = kh[:, :, :, :q1]
        vb = vh[:, :, :q1]
        sc = jnp.einsum('bhcqd,bhckd->bhcqk', qb, kb).astype(jnp.float32)
        causal = jnp.arange(q1)[None, :] <= jnp.arange(q0, q1)[:, None]
        sc = jnp.where(causal, sc, -jnp.inf)
        p = jax.nn.softmax(sc, axis=-1)
        a = p[:, :, 0] - lam * p[:, :, 1]
        outs.append(jnp.einsum('bhqk,bhkd->bhqd', a.astype(v.dtype), vb))
    o = jnp.concatenate(outs, axis=2)
    return jnp.transpose(o, (0, 2, 1, 3))


def causal_depthwise_conv(u, w):
    c = u.shape[-1]
    return lax.conv_general_dilated(
        u, w[:, None, :], window_strides=(1,), padding=((CONV_K - 1, 0),),
        dimension_numbers=('NWC', 'WIO', 'NWC'), feature_group_count=c)


def cross_attention(h, m, w_q, w_kv, w_o):
    b, s, _ = h.shape
    n_mem = m.shape[1]
    q = (h @ w_q).reshape(b, s, XATTN_HEADS, XATTN_HEAD_DIM)
    kv = (m @ w_kv).reshape(b, n_mem, 2, XATTN_HEADS, XATTN_HEAD_DIM)
    k, v = kv[:, :, 0], kv[:, :, 1]
    sc = jnp.einsum('bshd,bmhd->bhsm', q, k).astype(jnp.float32) * (XATTN_HEAD_DIM ** -0.5)
    p = jax.nn.softmax(sc, axis=-1)
    o = jnp.einsum('bhsm,bmhd->bshd', p.astype(v.dtype), v).reshape(b, s, XATTN_WIDTH)
    return o @ w_o


def setup_inputs(seed: int = 0) -> dict:
    key = jax.random.key(seed)
    ks = jax.random.split(key, 32)
    f = jnp.float32

    def w(k, shape, fan_in):
        return jax.random.normal(k, shape, f) * (fan_in ** -0.5)

    def g(k, shape):
        return 1.0 + 0.01 * jax.random.normal(k, shape, f)

    L, D = DEPTH, D_MODEL
    return {
        'x': jax.random.normal(ks[0], (BATCH, SEQ, D), f),
        'mem': jax.random.normal(ks[1], (BATCH, MEM_LEN, D), f),
        'ffn1_norm': g(ks[2], (L, D)),
        'ffn1_w_gate': w(ks[3], (L, D, D_FF), D),
        'ffn1_w_up': w(ks[4], (L, D, D_FF), D),
        'ffn1_w_down': w(ks[5], (L, D_FF, D), D_FF),
        'mix_norm': g(ks[6], (L, D)),
        'w_mix_in': w(ks[7], (L, D, MIX_IN_WIDTH), D),
        'b_gates': 0.1 * jax.random.normal(ks[8], (L, N_BRANCHES, D), f),
        'lambda_q1': 0.1 * jax.random.normal(ks[9], (L, DIFF_HEAD_DIM), f),
        'lambda_k1': 0.1 * jax.random.normal(ks[10], (L, DIFF_HEAD_DIM), f),
        'lambda_q2': 0.1 * jax.random.normal(ks[11], (L, DIFF_HEAD_DIM), f),
        'lambda_k2': 0.1 * jax.random.normal(ks[12], (L, DIFF_HEAD_DIM), f),
        'diff_subln': g(ks[13], (L, 2 * DIFF_HEAD_DIM)),
        'w_attn_out': w(ks[14], (L, DIFF_V_WIDTH, D), DIFF_V_WIDTH),
        'conv_w': w(ks[15], (L, CONV_K, CONV_WIDTH), CONV_K),
        'w_conv_out': w(ks[16], (L, CONV_WIDTH, D), CONV_WIDTH),
        'w_mix_out': w(ks[17], (L, D, D), D),
        'xattn_norm': g(ks[18], (L, D)),
        'mem_norm': g(ks[19], (L, D)),
        'w_xq': w(ks[20], (L, D, XATTN_WIDTH), D),
        'w_xkv': w(ks[21], (L, D, 2 * XATTN_WIDTH), D),
        'w_xo': w(ks[22], (L, XATTN_WIDTH, D), XATTN_WIDTH),
        'ffn2_norm': g(ks[23], (L, D)),
        'ffn2_w_gate': w(ks[24], (L, D, D_FF), D),
        'ffn2_w_up': w(ks[25], (L, D, D_FF), D),
        'ffn2_w_down': w(ks[26], (L, D_FF, D), D_FF),
        'final_norm': g(ks[27], (D,)),
    }


def reference(x, mem, ffn1_norm, ffn1_w_gate, ffn1_w_up, ffn1_w_down,
              mix_norm, w_mix_in, b_gates, lambda_q1, lambda_k1, lambda_q2, lambda_k2,
              diff_subln, w_attn_out, conv_w, w_conv_out, w_mix_out,
              xattn_norm, mem_norm, w_xq, w_xkv, w_xo,
              ffn2_norm, ffn2_w_gate, ffn2_w_up, ffn2_w_down, final_norm):
    b, s, _ = x.shape
    for l in range(DEPTH):
        x = x + 0.5 * swiglu(rms_norm(x, ffn1_norm[l]), ffn1_w_gate[l], ffn1_w_up[l], ffn1_w_down[l])

        h = rms_norm(x, mix_norm[l])
        z = h @ w_mix_in[l]
        q, k, v, gate_b, gate_c, u, ga_pre, gc_pre = jnp.split(z, MIX_IN_OFFSETS, axis=-1)

        lam_init = 0.8 - 0.6 * math.exp(-0.3 * l)
        lam = diff_lambda(lambda_q1[l], lambda_k1[l], lambda_q2[l], lambda_k2[l], lam_init)
        ya = diff_attention(q.reshape(b, s, DIFF_HEADS, 2, DIFF_HEAD_DIM),
                            k.reshape(b, s, DIFF_HEADS, 2, DIFF_HEAD_DIM),
                            v.reshape(b, s, DIFF_HEADS, 2 * DIFF_HEAD_DIM), lam)
        ya = rms_norm(ya, diff_subln[l], SUBLN_EPS) * (1.0 - lam_init)
        ya = ya.reshape(b, s, DIFF_V_WIDTH) @ w_attn_out[l]

        yc = (gate_b * causal_depthwise_conv(gate_c * u, conv_w[l])) @ w_conv_out[l]

        ga = jax.nn.sigmoid(ga_pre + b_gates[l, 0])
        gc = jax.nn.sigmoid(gc_pre + b_gates[l, 1])
        x = x + (ga * ya + gc * yc) @ w_mix_out[l]

        x = x + cross_attention(rms_norm(x, xattn_norm[l]), rms_norm(mem, mem_norm[l]),
                                w_xq[l], w_xkv[l], w_xo[l])

        x = x + 0.5 * swiglu(rms_norm(x, ffn2_norm[l]), ffn2_w_gate[l], ffn2_w_up[l], ffn2_w_down[l])
    return rms_norm(x, final_norm)
```

```python
import functools
import math

import jax
import jax.numpy as jnp
from jax import lax
from jax.experimental import pallas as pl
from jax.experimental.pallas import tpu as pltpu

D_MODEL = 2048
MEM_LEN = 256
DIFF_HEADS = 8
DIFF_HEAD_DIM = 128
DIFF_PAIR = 2 * DIFF_HEAD_DIM
CONV_K = 3
XATTN_HEADS = 4
XATTN_HEAD_DIM = 128
XATTN_WIDTH = XATTN_HEADS * XATTN_HEAD_DIM
NORM_EPS = 1e-6
SUBLN_EPS = 1e-5

V7X_LANES = 128
V7X_SUBLANES = 8
V7X_MXU_DIM = 256
V7X_VMEM_LIMIT_BYTES = 56 * 1024 * 1024

BF16 = jnp.bfloat16
F32 = jnp.float32


def _params(n_axes):
    return pltpu.CompilerParams(
        dimension_semantics=("arbitrary",) * n_axes,
        vmem_limit_bytes=V7X_VMEM_LIMIT_BYTES)


def _dot(a, b):
    return jnp.dot(a, b, preferred_element_type=F32)


def _dot_nt(a, b):
    return lax.dot_general(a, b, (((1,), (1,)), ((), ())), preferred_element_type=F32)


def _rms(x, g, eps):
    return x * lax.rsqrt(jnp.mean(x * x, axis=-1, keepdims=True) + eps) * g


def _sigmoid(x):
    return 1.0 / (1.0 + jnp.exp(-x))


def _ffn_kernel(x_ref, g_ref, wg_ref, wu_ref, wd_ref, g2_ref, *refs, emit_y):
    if emit_y:
        y_ref, n_ref, h_ref, acc_ref = refs
    else:
        n_ref, h_ref, acc_ref = refs
    j = pl.program_id(1)

    @pl.when(j == 0)
    def _():
        h_ref[...] = _rms(x_ref[...], g_ref[...], NORM_EPS).astype(BF16)
        acc_ref[...] = jnp.zeros_like(acc_ref)

    h = h_ref[...]
    gate = _dot(h, wg_ref[...])
    up = _dot(h, wu_ref[...])
    act = (gate * _sigmoid(gate) * up).astype(BF16)
    acc_ref[...] += _dot(act, wd_ref[...])

    @pl.when(j == pl.num_programs(1) - 1)
    def _():
        y = x_ref[...] + 0.5 * acc_ref[...]
        if emit_y:
            y_ref[...] = y
        n_ref[...] = _rms(y, g2_ref[...], NORM_EPS).astype(n_ref.dtype)


def _ffn(x, g, wg, wu, wd, g2, *, emit_y, tm, tf):
    t, d = x.shape
    f = wg.shape[1]
    grid = (t // tm, f // tf)
    row = pl.BlockSpec((tm, d), lambda i, j: (i, 0))
    vec = pl.BlockSpec((1, d), lambda i, j: (0, 0))
    in_specs = [row, vec,
                pl.BlockSpec((d, tf), lambda i, j: (0, j)),
                pl.BlockSpec((d, tf), lambda i, j: (0, j)),
                pl.BlockSpec((tf, d), lambda i, j: (j, 0)),
                vec]
    if emit_y:
        out_shape = (jax.ShapeDtypeStruct((t, d), F32), jax.ShapeDtypeStruct((t, d), BF16))
        out_specs = (row, row)
    else:
        out_shape = jax.ShapeDtypeStruct((t, d), F32)
        out_specs = row
    return pl.pallas_call(
        functools.partial(_ffn_kernel, emit_y=emit_y),
        grid=grid, in_specs=in_specs, out_specs=out_specs, out_shape=out_shape,
        scratch_shapes=[pltpu.VMEM((tm, d), BF16), pltpu.VMEM((tm, d), F32)],
        compiler_params=_params(2),
        name="ffn_y" if emit_y else "ffn_final",
    )(x, g, wg, wu, wd, g2)


def _qkv_kernel(h_ref, w_ref, o_ref, *, q_blocks, scale):
    j = pl.program_id(1)
    acc = _dot(h_ref[...], w_ref[...])
    acc = acc * jnp.where(j < q_blocks, scale, 1.0).astype(F32)
    o_ref[...] = acc.astype(o_ref.dtype)


def _qkv(h, w, *, tm, tn):
    t, d = h.shape
    n = 3 * d
    return pl.pallas_call(
        functools.partial(_qkv_kernel, q_blocks=d // tn, scale=DIFF_HEAD_DIM ** -0.5),
        grid=(t // tm, n // tn),
        in_specs=[pl.BlockSpec((tm, d), lambda i, j: (i, 0)),
                  pl.BlockSpec((d, tn), lambda i, j: (0, j))],
        out_specs=pl.BlockSpec((tm, tn), lambda i, j: (i, j)),
        out_shape=jax.ShapeDtypeStruct((t, n), BF16),
        compiler_params=_params(2),
        name="qkv_proj",
    )(h, w)


def _conv_kernel(h_ref, wb_ref, wc_ref, wu_ref, cw_ref, o_ref, carry_ref, *, tiles_per_seq):
    i = pl.program_id(1)
    h = h_ref[...]
    cu = _dot(h, wc_ref[...]) * _dot(h, wu_ref[...])
    tm = cu.shape[0]

    @pl.when(i % tiles_per_seq == 0)
    def _():
        carry_ref[...] = jnp.zeros_like(carry_ref)

    prev = carry_ref[...]
    carry_ref[...] = cu[tm - V7X_SUBLANES:, :]

    row8 = lax.broadcasted_iota(jnp.int32, prev.shape, 0)

    def shifted(s):
        rolled = pltpu.roll(cu, s, axis=0)
        head = jnp.where(row8 < s, pltpu.roll(prev, s, axis=0), rolled[:V7X_SUBLANES, :])
        return jnp.concatenate([head, rolled[V7X_SUBLANES:, :]], axis=0)

    cw = cw_ref[...]
    y = cw[2:3, :] * cu + cw[1:2, :] * shifted(1) + cw[0:1, :] * shifted(2)
    o_ref[...] = (_dot(h, wb_ref[...]) * y).astype(o_ref.dtype)


def _conv_branch(h, w, conv_w, *, seq, tm, tn):
    t, d = h.shape
    nb = d // tn
    return pl.pallas_call(
        functools.partial(_conv_kernel, tiles_per_seq=seq // tm),
        grid=(nb, t // tm),
        in_specs=[pl.BlockSpec((tm, d), lambda j, i: (i, 0)),
                  pl.BlockSpec((d, tn), lambda j, i: (0, 3 * nb + j)),
                  pl.BlockSpec((d, tn), lambda j, i: (0, 4 * nb + j)),
                  pl.BlockSpec((d, tn), lambda j, i: (0, 5 * nb + j)),
                  pl.BlockSpec((CONV_K, tn), lambda j, i: (0, j))],
        out_specs=pl.BlockSpec((tm, tn), lambda j, i: (i, j)),
        out_shape=jax.ShapeDtypeStruct((t, d), BF16),
        scratch_shapes=[pltpu.VMEM((V7X_SUBLANES, tn), F32)],
        compiler_params=_params(2),
        name="conv_branch",
    )(h, w, w, w, conv_w)


def _diff_attn_kernel(q_ref, k_ref, v_ref, lq1_ref, lk1_ref, lq2_ref, lk2_ref, sg_ref, o_ref,
                      acc1_ref, acc2_ref, *, tq, lam_init):
    qi = pl.program_id(2)
    hd = DIFF_HEAD_DIM
    q = q_ref[...]
    q1, q2 = q[:, :hd], q[:, hd:]

    def step(j, carry, masked):
        m1, l1, m2, l2 = carry
        off = pl.multiple_of(j * tq, tq)
        k = k_ref[pl.ds(off, tq), :]
        v = v_ref[pl.ds(off, tq), :]
        s1 = _dot_nt(q1, k[:, :hd])
        s2 = _dot_nt(q2, k[:, hd:])
        if masked:
            keep = (lax.broadcasted_iota(jnp.int32, s1.shape, 1)
                    <= lax.broadcasted_iota(jnp.int32, s1.shape, 0))
            s1 = jnp.where(keep, s1, -jnp.inf)
            s2 = jnp.where(keep, s2, -jnp.inf)

        def online(s, m, l, acc_ref):
            m_new = jnp.maximum(m, jnp.max(s, axis=-1, keepdims=True))
            alpha = jnp.exp(m - m_new)
            p = jnp.exp(s - m_new)
            l_new = alpha * l + jnp.sum(p, axis=-1, keepdims=True)
            acc_ref[...] = alpha * acc_ref[...] + _dot(p.astype(BF16), v)
            return m_new, l_new

        m1, l1 = online(s1, m1, l1, acc1_ref)
        m2, l2 = online(s2, m2, l2, acc2_ref)
        return m1, l1, m2, l2

    acc1_ref[...] = jnp.zeros_like(acc1_ref)
    acc2_ref[...] = jnp.zeros_like(acc2_ref)
    neg = jnp.full((tq, 1), -jnp.inf, F32)
    zero = jnp.zeros((tq, 1), F32)
    carry = step(qi, (neg, zero, neg, zero), True)
    carry = lax.fori_loop(0, qi, lambda j, c: step(j, c, False), carry)
    _, l1, _, l2 = carry

    lam = (jnp.exp(jnp.sum(lq1_ref[...] * lk1_ref[...], axis=-1, keepdims=True))
           - jnp.exp(jnp.sum(lq2_ref[...] * lk2_ref[...], axis=-1, keepdims=True))
           + lam_init)
    o = acc1_ref[...] / l1 - lam * (acc2_ref[...] / l2)
    o = _rms(o, sg_ref[...], SUBLN_EPS) * (1.0 - lam_init)
    o_ref[...] = o.astype(o_ref.dtype)


def _diff_attn(zqkv, lq1, lk1, lq2, lk2, subln, *, batch, seq, tq, lam_init):
    t = zqkv.shape[0]
    nq = seq // tq
    w = DIFF_PAIR
    lvec = pl.BlockSpec((1, DIFF_HEAD_DIM), lambda b, h, qi: (0, 0))
    return pl.pallas_call(
        functools.partial(_diff_attn_kernel, tq=tq, lam_init=lam_init),
        grid=(batch, DIFF_HEADS, nq),
        in_specs=[pl.BlockSpec((tq, w), lambda b, h, qi: (b * nq + qi, h)),
                  pl.BlockSpec((seq, w), lambda b, h, qi: (b, DIFF_HEADS + h)),
                  pl.BlockSpec((seq, w), lambda b, h, qi: (b, 2 * DIFF_HEADS + h)),
                  lvec, lvec, lvec, lvec,
                  pl.BlockSpec((1, w), lambda b, h, qi: (0, 0))],
        out_specs=pl.BlockSpec((tq, w), lambda b, h, qi: (b * nq + qi, h)),
        out_shape=jax.ShapeDtypeStruct((t, DIFF_HEADS * w), BF16),
        scratch_shapes=[pltpu.VMEM((tq, w), F32), pltpu.VMEM((tq, w), F32)],
        compiler_params=_params(3),
        name="diff_attn",
    )(zqkv, zqkv, zqkv, lq1, lk1, lq2, lk2, subln)


def _merge_kernel(ya_ref, cv_ref, h_ref, wa_ref, wc_ref, wga_ref, wgc_ref, b_ref, o_ref):
    h = h_ref[...]
    b = b_ref[...]
    ga = _sigmoid(_dot(h, wga_ref[...]) + b[0:1, :])
    gc = _sigmoid(_dot(h, wgc_ref[...]) + b[1:2, :])
    m = ga * _dot(ya_ref[...], wa_ref[...]) + gc * _dot(cv_ref[...], wc_ref[...])
    o_ref[...] = m.astype(o_ref.dtype)


def _merge(ya, cv, h, wa, wc, w_mix_in, b_gates, *, tm, tn):
    t, d = h.shape
    nb = d // tn
    row = pl.BlockSpec((tm, d), lambda i, j: (i, 0))
    col = pl.BlockSpec((d, tn), lambda i, j: (0, j))
    return pl.pallas_call(
        _merge_kernel,
        grid=(t // tm, nb),
        in_specs=[row, row, row, col, col,
                  pl.BlockSpec((d, tn), lambda i, j: (0, 6 * nb + j)),
                  pl.BlockSpec((d, tn), lambda i, j: (0, 7 * nb + j)),
                  pl.BlockSpec((2, tn), lambda i, j: (0, j))],
        out_specs=pl.BlockSpec((tm, tn), lambda i, j: (i, j)),
        out_shape=jax.ShapeDtypeStruct((t, d), BF16),
        compiler_params=_params(2),
        name="gated_merge",
    )(ya, cv, h, wa, wc, w_mix_in, w_mix_in, b_gates)


def _mix_out_kernel(m_ref, w_ref, x_ref, g_ref, y_ref, n_ref):
    y = x_ref[...] + _dot(m_ref[...], w_ref[...])
    y_ref[...] = y
    n_ref[...] = _rms(y, g_ref[...], NORM_EPS).astype(n_ref.dtype)


def _mix_out(m, w, x, g, *, tm):
    t, d = x.shape
    row = pl.BlockSpec((tm, d), lambda i: (i, 0))
    return pl.pallas_call(
        _mix_out_kernel,
        grid=(t // tm,),
        in_specs=[row, pl.BlockSpec((d, d), lambda i: (0, 0)), row,
                  pl.BlockSpec((1, d), lambda i: (0, 0))],
        out_specs=(row, row),
        out_shape=(jax.ShapeDtypeStruct((t, d), F32), jax.ShapeDtypeStruct((t, d), BF16)),
        compiler_params=_params(1),
        name="mix_out",
    )(m, w, x, g)


def _mem_kv_kernel(m_ref, g_ref, w_ref, o_ref):
    hm = _rms(m_ref[...], g_ref[...], NORM_EPS).astype(BF16)
    o_ref[...] = _dot(hm, w_ref[...]).astype(o_ref.dtype)


def _mem_kv(mem, g, w, *, tm):
    t, d = mem.shape
    n = w.shape[1]
    return pl.pallas_call(
        _mem_kv_kernel,
        grid=(t // tm,),
        in_specs=[pl.BlockSpec((tm, d), lambda i: (i, 0)),
                  pl.BlockSpec((1, d), lambda i: (0, 0)),
                  pl.BlockSpec((d, n), lambda i: (0, 0))],
        out_specs=pl.BlockSpec((tm, n), lambda i: (i, 0)),
        out_shape=jax.ShapeDtypeStruct((t, n), BF16),
        compiler_params=_params(1),
        name="mem_kv",
    )(mem, g, w)


def _xattn_kernel(n_ref, x_ref, wq_ref, kv_ref, wo_ref, y_ref):
    hd = XATTN_HEAD_DIM
    q = (_dot(n_ref[...], wq_ref[...]) * (hd ** -0.5)).astype(BF16)
    kv = kv_ref[...]
    outs = []
    for h in range(XATTN_HEADS):
        k = kv[:, h * hd:(h + 1) * hd]
        v = kv[:, XATTN_WIDTH + h * hd:XATTN_WIDTH + (h + 1) * hd]
        s = _dot_nt(q[:, h * hd:(h + 1) * hd], k)
        p = jnp.exp(s - jnp.max(s, axis=-1, keepdims=True))
        l = jnp.sum(p, axis=-1, keepdims=True)
        outs.append((_dot(p.astype(BF16), v) / l).astype(BF16))
    o = jnp.concatenate(outs, axis=-1)
    y_ref[...] = x_ref[...] + _dot(o, wo_ref[...])


def _xattn(n, x, wq, kv, wo, *, seq, tm):
    t, d = x.shape
    tiles_per_seq = seq // tm
    row = pl.BlockSpec((tm, d), lambda i: (i, 0))
    return pl.pallas_call(
        _xattn_kernel,
        grid=(t // tm,),
        in_specs=[row, row,
                  pl.BlockSpec((d, XATTN_WIDTH), lambda i: (0, 0)),
                  pl.BlockSpec((MEM_LEN, 2 * XATTN_WIDTH), lambda i: (i // tiles_per_seq, 0)),
                  pl.BlockSpec((XATTN_WIDTH, d), lambda i: (0, 0))],
        out_specs=row,
        out_shape=jax.ShapeDtypeStruct((t, d), F32),
        compiler_params=_params(1),
        name="xattn",
    )(n, x, wq, kv, wo)


def _pad_to(w, axis, size):
    pad = [(0, 0)] * w.ndim
    pad[axis] = (0, size - w.shape[axis])
    return jnp.pad(w, pad)


def kernel(x, mem, ffn1_norm, ffn1_w_gate, ffn1_w_up, ffn1_w_down, mix_norm, w_mix_in, b_gates,
           lambda_q1, lambda_k1, lambda_q2, lambda_k2, diff_subln, w_attn_out, conv_w, w_conv_out,
           w_mix_out, xattn_norm, mem_norm, w_xq, w_xkv, w_xo, ffn2_norm, ffn2_w_gate, ffn2_w_up,
           ffn2_w_down, final_norm):
    b, s, d = x.shape
    depth = ffn1_norm.shape[0]
    t = b * s
    ffn_tf = 2 * V7X_MXU_DIM
    f_pad = pl.cdiv(ffn1_w_gate.shape[-1], ffn_tf) * ffn_tf

    xt = x.reshape(t, d)
    memt = mem.reshape(b * mem.shape[1], d)
    out = None
    for l in range(depth):
        last = l == depth - 1
        row = lambda a: a[l].reshape(1, -1)
        ffn_w = lambda wg, wu, wd: (_pad_to(wg[l].astype(BF16), 1, f_pad),
                                    _pad_to(wu[l].astype(BF16), 1, f_pad),
                                    _pad_to(wd[l].astype(BF16), 0, f_pad))
        lam_init = 0.8 - 0.6 * math.exp(-0.3 * l)

        x1, h = _ffn(xt, row(ffn1_norm), *ffn_w(ffn1_w_gate, ffn1_w_up, ffn1_w_down),
                     row(mix_norm), emit_y=True, tm=512, tf=ffn_tf)

        w_in = w_mix_in[l].astype(BF16)
        zqkv = _qkv(h, w_in, tm=1024, tn=1024)
        cv = _conv_branch(h, w_in, conv_w[l], seq=s, tm=512, tn=512)
        ya = _diff_attn(zqkv, row(lambda_q1), row(lambda_k1), row(lambda_q2), row(lambda_k2),
                        row(diff_subln), batch=b, seq=s, tq=256, lam_init=lam_init)
        m = _merge(ya, cv, h, w_attn_out[l].astype(BF16), w_conv_out[l].astype(BF16), w_in,
                   b_gates[l], tm=1024, tn=512)
        x2, n2 = _mix_out(m, w_mix_out[l].astype(BF16), x1, row(xattn_norm), tm=512)

        kv = _mem_kv(memt, row(mem_norm), w_xkv[l].astype(BF16), tm=512)
        x3 = _xattn(n2, x2, w_xq[l].astype(BF16), kv, w_xo[l].astype(BF16), seq=s, tm=512)

        g_next = final_norm.reshape(1, -1) if last else ffn1_norm[l + 1].reshape(1, -1)
        res = _ffn(x3, row(ffn2_norm), *ffn_w(ffn2_w_gate, ffn2_w_up, ffn2_w_down),
                   g_next, emit_y=not last, tm=512, tf=ffn_tf)
        if last:
            out = res
        else:
            xt = res[0]
    return out.reshape(b, s, d)
```

```python
import functools
import math

import jax
import jax.numpy as jnp
from jax import lax
from jax.experimental import pallas as pl
from jax.experimental.pallas import tpu as pltpu

D_MODEL = 2048
MEM_LEN = 256
DIFF_HEADS = 8
DIFF_HEAD_DIM = 128
DIFF_PAIR = 2 * DIFF_HEAD_DIM
CONV_K = 3
XATTN_HEADS = 4
XATTN_HEAD_DIM = 128
XATTN_WIDTH = XATTN_HEADS * XATTN_HEAD_DIM
NORM_EPS = 1e-6
SUBLN_EPS = 1e-5

V7X_LANES = 128
V7X_SUBLANES = 8
V7X_MXU_DIM = 256
V7X_VMEM_LIMIT_BYTES = 56 * 1024 * 1024

BF16 = jnp.bfloat16
F32 = jnp.float32


def _params(n_axes):
    return pltpu.CompilerParams(
        dimension_semantics=("arbitrary",) * n_axes,
        vmem_limit_bytes=V7X_VMEM_LIMIT_BYTES)


def _dot(a, b):
    return jnp.dot(a, b, preferred_element_type=F32)


def _dot_nt(a, b):
    return lax.dot_general(a, b, (((1,), (1,)), ((), ())), preferred_element_type=F32)


def _rms(x, g, eps):
    return x * lax.rsqrt(jnp.mean(x * x, axis=-1, keepdims=True) + eps) * g


def _sigmoid(x):
    return 1.0 / (1.0 + jnp.exp(-x))


def _cast_weights_once(step, pairs):
    @pl.when(step == 0)
    def _():
        for src, dst in pairs:
            dst[...] = src[...].astype(dst.dtype)


def _cast_pad_kernel(wg_ref, wu_ref, wd_ref, og_ref, ou_ref, od_ref, *, f, tf):
    valid = f - pl.program_id(0) * tf
    col_ok = lax.broadcasted_iota(jnp.int32, wg_ref.shape, 1) < valid
    row_ok = lax.broadcasted_iota(jnp.int32, wd_ref.shape, 0) < valid
    og_ref[...] = jnp.where(col_ok, wg_ref[...], 0.0).astype(BF16)
    ou_ref[...] = jnp.where(col_ok, wu_ref[...], 0.0).astype(BF16)
    od_ref[...] = jnp.where(row_ok, wd_ref[...], 0.0).astype(BF16)


def _cast_pad_ffn_weights(wg, wu, wd, *, tf):
    d, f = wg.shape
    nf = pl.cdiv(f, tf)
    col = pl.BlockSpec((d, tf), lambda j: (0, j))
    row = pl.BlockSpec((tf, d), lambda j: (j, 0))
    return pl.pallas_call(
        functools.partial(_cast_pad_kernel, f=f, tf=tf),
        grid=(nf,),
        in_specs=[col, col, row],
        out_specs=(col, col, row),
        out_shape=(jax.ShapeDtypeStruct((d, nf * tf), BF16),
                   jax.ShapeDtypeStruct((d, nf * tf), BF16),
                   jax.ShapeDtypeStruct((nf * tf, d), BF16)),
        compiler_params=_params(1),
        name="ffn_weight_cast",
    )(wg, wu, wd)


def _ffn_kernel(x_ref, g_ref, wg_ref, wu_ref, wd_ref, g2_ref, *refs, emit_y):
    if emit_y:
        y_ref, n_ref, h_ref, acc_ref = refs
    else:
        n_ref, h_ref, acc_ref = refs
    j = pl.program_id(1)

    @pl.when(j == 0)
    def _():
        h_ref[...] = _rms(x_ref[...], g_ref[...], NORM_EPS).astype(BF16)
        acc_ref[...] = jnp.zeros_like(acc_ref)

    h = h_ref[...]
    gate = _dot(h, wg_ref[...])
    up = _dot(h, wu_ref[...])
    act = (gate * _sigmoid(gate) * up).astype(BF16)
    acc_ref[...] += _dot(act, wd_ref[...])

    @pl.when(j == pl.num_programs(1) - 1)
    def _():
        y = x_ref[...] + 0.5 * acc_ref[...]
        if emit_y:
            y_ref[...] = y
        n_ref[...] = _rms(y, g2_ref[...], NORM_EPS).astype(n_ref.dtype)


def _ffn(x, g, wg, wu, wd, g2, *, emit_y, tm, tf):
    t, d = x.shape
    f = wg.shape[1]
    grid = (t // tm, f // tf)
    row = pl.BlockSpec((tm, d), lambda i, j: (i, 0))
    vec = pl.BlockSpec((1, d), lambda i, j: (0, 0))
    in_specs = [row, vec,
                pl.BlockSpec((d, tf), lambda i, j: (0, j)),
                pl.BlockSpec((d, tf), lambda i, j: (0, j)),
                pl.BlockSpec((tf, d), lambda i, j: (j, 0)),
                vec]
    if emit_y:
        out_shape = (jax.ShapeDtypeStruct((t, d), F32), jax.ShapeDtypeStruct((t, d), BF16))
        out_specs = (row, row)
    else:
        out_shape = jax.ShapeDtypeStruct((t, d), F32)
        out_specs = row
    return pl.pallas_call(
        functools.partial(_ffn_kernel, emit_y=emit_y),
        grid=grid, in_specs=in_specs, out_specs=out_specs, out_shape=out_shape,
        scratch_shapes=[pltpu.VMEM((tm, d), BF16), pltpu.VMEM((tm, d), F32)],
        compiler_params=_params(2),
        name="ffn_y" if emit_y else "ffn_final",
    )(x, g, wg, wu, wd, g2)


def _qkv_kernel(h_ref, w_ref, o_ref, wb_ref, *, q_blocks, scale):
    j = pl.program_id(0)
    _cast_weights_once(pl.program_id(1), [(w_ref, wb_ref)])
    acc = _dot(h_ref[...], wb_ref[...])
    acc = acc * jnp.where(j < q_blocks, scale, 1.0).astype(F32)
    o_ref[...] = acc.astype(o_ref.dtype)


def _qkv(h, w, *, tm, tn):
    t, d = h.shape
    n = 3 * d
    return pl.pallas_call(
        functools.partial(_qkv_kernel, q_blocks=d // tn, scale=DIFF_HEAD_DIM ** -0.5),
        grid=(n // tn, t // tm),
        in_specs=[pl.BlockSpec((tm, d), lambda j, i: (i, 0)),
                  pl.BlockSpec((d, tn), lambda j, i: (0, j))],
        out_specs=pl.BlockSpec((tm, tn), lambda j, i: (i, j)),
        out_shape=jax.ShapeDtypeStruct((t, n), BF16),
        scratch_shapes=[pltpu.VMEM((d, tn), BF16)],
        compiler_params=_params(2),
        name="qkv_proj",
    )(h, w)


def _conv_kernel(h_ref, wb_ref, wc_ref, wu_ref, cw_ref, o_ref, carry_ref, wbb_ref, wcb_ref, wub_ref,
                 *, tiles_per_seq):
    i = pl.program_id(1)
    _cast_weights_once(i, [(wb_ref, wbb_ref), (wc_ref, wcb_ref), (wu_ref, wub_ref)])
    h = h_ref[...]
    cu = _dot(h, wcb_ref[...]) * _dot(h, wub_ref[...])
    tm = cu.shape[0]

    @pl.when(i % tiles_per_seq == 0)
    def _():
        carry_ref[...] = jnp.zeros_like(carry_ref)

    prev = carry_ref[...]
    carry_ref[...] = cu[tm - V7X_SUBLANES:, :]

    row8 = lax.broadcasted_iota(jnp.int32, prev.shape, 0)

    def shifted(s):
        rolled = pltpu.roll(cu, s, axis=0)
        head = jnp.where(row8 < s, pltpu.roll(prev, s, axis=0), rolled[:V7X_SUBLANES, :])
        return jnp.concatenate([head, rolled[V7X_SUBLANES:, :]], axis=0)

    cw = cw_ref[...]
    y = cw[2:3, :] * cu + cw[1:2, :] * shifted(1) + cw[0:1, :] * shifted(2)
    o_ref[...] = (_dot(h, wbb_ref[...]) * y).astype(o_ref.dtype)


def _conv_branch(h, w, conv_w, *, seq, tm, tn):
    t, d = h.shape
    nb = d // tn
    wspec = lambda k: pl.BlockSpec((d, tn), lambda j, i: (0, k * nb + j))
    return pl.pallas_call(
        functools.partial(_conv_kernel, tiles_per_seq=seq // tm),
        grid=(nb, t // tm),
        in_specs=[pl.BlockSpec((tm, d), lambda j, i: (i, 0)),
                  wspec(3), wspec(4), wspec(5),
                  pl.BlockSpec((CONV_K, tn), lambda j, i: (0, j))],
        out_specs=pl.BlockSpec((tm, tn), lambda j, i: (i, j)),
        out_shape=jax.ShapeDtypeStruct((t, d), BF16),
        scratch_shapes=[pltpu.VMEM((V7X_SUBLANES, tn), F32)] + [pltpu.VMEM((d, tn), BF16)] * 3,
        compiler_params=_params(2),
        name="conv_branch",
    )(h, w, w, w, conv_w)


def _diff_attn_kernel(q_ref, k_ref, v_ref, lq1_ref, lk1_ref, lq2_ref, lk2_ref, sg_ref, o_ref,
                      *, tq, nq, lam_init):
    hd = DIFF_HEAD_DIM
    lam = (jnp.exp(jnp.sum(lq1_ref[...] * lk1_ref[...], axis=-1, keepdims=True))
           - jnp.exp(jnp.sum(lq2_ref[...] * lk2_ref[...], axis=-1, keepdims=True))
           + lam_init)
    sg = sg_ref[...]
    keep = (lax.broadcasted_iota(jnp.int32, (tq, tq), 1)
            <= lax.broadcasted_iota(jnp.int32, (tq, tq), 0))

    def softmax_terms(qc, kc, r0):
        s = _dot_nt(qc, kc)
        diag = jnp.where(keep, s[:, r0:], -jnp.inf)
        s = jnp.concatenate([s[:, :r0], diag], axis=1) if r0 else diag
        p = jnp.exp(s - jnp.max(s, axis=-1, keepdims=True))
        return p, jnp.sum(p, axis=-1, keepdims=True)

    for qi in range(nq):
        r0 = qi * tq
        q = q_ref[r0:r0 + tq, :]
        k = k_ref[0:r0 + tq, :]
        p1, l1 = softmax_terms(q[:, :hd], k[:, :hd], r0)
        p2, l2 = softmax_terms(q[:, hd:], k[:, hd:], r0)
        a = p1 * (1.0 / l1) - p2 * (lam / l2)
        o = _dot(a.astype(BF16), v_ref[0:r0 + tq, :])
        o = _rms(o, sg, SUBLN_EPS) * (1.0 - lam_init)
        o_ref[r0:r0 + tq, :] = o.astype(o_ref.dtype)


def _diff_attn(zqkv, lq1, lk1, lq2, lk2, subln, *, batch, seq, tq, lam_init):
    t = zqkv.shape[0]
    w = DIFF_PAIR
    lvec = pl.BlockSpec((1, DIFF_HEAD_DIM), lambda b, h: (0, 0))
    return pl.pallas_call(
        functools.partial(_diff_attn_kernel, tq=tq, nq=seq // tq, lam_init=lam_init),
        grid=(batch, DIFF_HEADS),
        in_specs=[pl.BlockSpec((seq, w), lambda b, h: (b, h)),
                  pl.BlockSpec((seq, w), lambda b, h: (b, DIFF_HEADS + h)),
                  pl.BlockSpec((seq, w), lambda b, h: (b, 2 * DIFF_HEADS + h)),
                  lvec, lvec, lvec, lvec,
                  pl.BlockSpec((1, w), lambda b, h: (0, 0))],
        out_specs=pl.BlockSpec((seq, w), lambda b, h: (b, h)),
        out_shape=jax.ShapeDtypeStruct((t, DIFF_HEADS * w), BF16),
        compiler_params=_params(2),
        name="diff_attn",
    )(zqkv, zqkv, zqkv, lq1, lk1, lq2, lk2, subln)


def _merge_kernel(ya_ref, cv_ref, h_ref, wa_ref, wc_ref, wga_ref, wgc_ref, b_ref, o_ref,
                  wab_ref, wcb_ref, wgab_ref, wgcb_ref):
    _cast_weights_once(pl.program_id(1), [(wa_ref, wab_ref), (wc_ref, wcb_ref),
                                          (wga_ref, wgab_ref), (wgc_ref, wgcb_ref)])
    h = h_ref[...]
    b = b_ref[...]
    ga = _sigmoid(_dot(h, wgab_ref[...]) + b[0:1, :])
    gc = _sigmoid(_dot(h, wgcb_ref[...]) + b[1:2, :])
    m = ga * _dot(ya_ref[...], wab_ref[...]) + gc * _dot(cv_ref[...], wcb_ref[...])
    o_ref[...] = m.astype(o_ref.dtype)


def _merge(ya, cv, h, wa, wc, w_mix_in, b_gates, *, tm, tn):
    t, d = h.shape
    nb = d // tn
    row = pl.BlockSpec((tm, d), lambda j, i: (i, 0))
    col = pl.BlockSpec((d, tn), lambda j, i: (0, j))
    return pl.pallas_call(
        _merge_kernel,
        grid=(nb, t // tm),
        in_specs=[row, row, row, col, col,
                  pl.BlockSpec((d, tn), lambda j, i: (0, 6 * nb + j)),
                  pl.BlockSpec((d, tn), lambda j, i: (0, 7 * nb + j)),
                  pl.BlockSpec((2, tn), lambda j, i: (0, j))],
        out_specs=pl.BlockSpec((tm, tn), lambda j, i: (i, j)),
        out_shape=jax.ShapeDtypeStruct((t, d), BF16),
        scratch_shapes=[pltpu.VMEM((d, tn), BF16)] * 4,
        compiler_params=_params(2),
        name="gated_merge",
    )(ya, cv, h, wa, wc, w_mix_in, w_mix_in, b_gates)


def _mix_out_kernel(m_ref, w_ref, x_ref, y_ref, wb_ref):
    _cast_weights_once(pl.program_id(1), [(w_ref, wb_ref)])
    y_ref[...] = x_ref[...] + _dot(m_ref[...], wb_ref[...])


def _mix_out(m, w, x, *, tm, tn):
    t, d = x.shape
    return pl.pallas_call(
        _mix_out_kernel,
        grid=(d // tn, t // tm),
        in_specs=[pl.BlockSpec((tm, d), lambda j, i: (i, 0)),
                  pl.BlockSpec((d, tn), lambda j, i: (0, j)),
                  pl.BlockSpec((tm, tn), lambda j, i: (i, j))],
        out_specs=pl.BlockSpec((tm, tn), lambda j, i: (i, j)),
        out_shape=jax.ShapeDtypeStruct((t, d), F32),
        scratch_shapes=[pltpu.VMEM((d, tn), BF16)],
        compiler_params=_params(2),
        name="mix_out",
    )(m, w, x)


def _mem_kv_kernel(m_ref, g_ref, w_ref, o_ref, wb_ref):
    _cast_weights_once(pl.program_id(0), [(w_ref, wb_ref)])
    hm = _rms(m_ref[...], g_ref[...], NORM_EPS).astype(BF16)
    o_ref[...] = _dot(hm, wb_ref[...]).astype(o_ref.dtype)


def _mem_kv(mem, g, w, *, tm):
    t, d = mem.shape
    n = w.shape[1]
    return pl.pallas_call(
        _mem_kv_kernel,
        grid=(t // tm,),
        in_specs=[pl.BlockSpec((tm, d), lambda i: (i, 0)),
                  pl.BlockSpec((1, d), lambda i: (0, 0)),
                  pl.BlockSpec((d, n), lambda i: (0, 0))],
        out_specs=pl.BlockSpec((tm, n), lambda i: (i, 0)),
        out_shape=jax.ShapeDtypeStruct((t, n), BF16),
        scratch_shapes=[pltpu.VMEM((d, n), BF16)],
        compiler_params=_params(1),
        name="mem_kv",
    )(mem, g, w)


def _xattn_kernel(x_ref, g_ref, wq_ref, kv_ref, wo_ref, y_ref, wqb_ref, wob_ref):
    _cast_weights_once(pl.program_id(0), [(wq_ref, wqb_ref), (wo_ref, wob_ref)])
    hd = XATTN_HEAD_DIM
    x = x_ref[...]
    n = _rms(x, g_ref[...], NORM_EPS).astype(BF16)
    q = (_dot(n, wqb_ref[...]) * (hd ** -0.5)).astype(BF16)
    kv = kv_ref[...]
    outs = []
    for h in range(XATTN_HEADS):
        k = kv[:, h * hd:(h + 1) * hd]
        v = kv[:, XATTN_WIDTH + h * hd:XATTN_WIDTH + (h + 1) * hd]
        s = _dot_nt(q[:, h * hd:(h + 1) * hd], k)
        p = jnp.exp(s - jnp.max(s, axis=-1, keepdims=True))
        l = jnp.sum(p, axis=-1, keepdims=True)
        outs.append((_dot(p.astype(BF16), v) / l).astype(BF16))
    o = jnp.concatenate(outs, axis=-1)
    y_ref[...] = x + _dot(o, wob_ref[...])


def _xattn(x, g, wq, kv, wo, *, seq, tm):
    t, d = x.shape
    tiles_per_seq = seq // tm
    row = pl.BlockSpec((tm, d), lambda i: (i, 0))
    return pl.pallas_call(
        _xattn_kernel,
        grid=(t // tm,),
        in_specs=[row,
                  pl.BlockSpec((1, d), lambda i: (0, 0)),
                  pl.BlockSpec((d, XATTN_WIDTH), lambda i: (0, 0)),
                  pl.BlockSpec((MEM_LEN, 2 * XATTN_WIDTH), lambda i: (i // tiles_per_seq, 0)),
                  pl.BlockSpec((XATTN_WIDTH, d), lambda i: (0, 0))],
        out_specs=row,
        out_shape=jax.ShapeDtypeStruct((t, d), F32),
        scratch_shapes=[pltpu.VMEM((d, XATTN_WIDTH), BF16), pltpu.VMEM((XATTN_WIDTH, d), BF16)],
        compiler_params=_params(1),
        name="xattn",
    )(x, g, wq, kv, wo)


def kernel(x, mem, ffn1_norm, ffn1_w_gate, ffn1_w_up, ffn1_w_down, mix_norm, w_mix_in, b_gates,
           lambda_q1, lambda_k1, lambda_q2, lambda_k2, diff_subln, w_attn_out, conv_w, w_conv_out,
           w_mix_out, xattn_norm, mem_norm, w_xq, w_xkv, w_xo, ffn2_norm, ffn2_w_gate, ffn2_w_up,
           ffn2_w_down, final_norm):
    b, s, d = x.shape
    depth = ffn1_norm.shape[0]
    t = b * s
    ffn_tf = 2 * V7X_MXU_DIM

    xt = x.reshape(t, d)
    memt = mem.reshape(b * mem.shape[1], d)
    out = None
    for l in range(depth):
        last = l == depth - 1
        row = lambda a: a[l].reshape(1, -1)
        lam_init = 0.8 - 0.6 * math.exp(-0.3 * l)

        w1 = _cast_pad_ffn_weights(ffn1_w_gate[l], ffn1_w_up[l], ffn1_w_down[l], tf=ffn_tf)
        x1, h = _ffn(xt, row(ffn1_norm), *w1, row(mix_norm), emit_y=True, tm=512, tf=ffn_tf)

        w_in = w_mix_in[l]
        zqkv = _qkv(h, w_in, tm=1024, tn=1024)
        cv = _conv_branch(h, w_in, conv_w[l], seq=s, tm=512, tn=512)
        ya = _diff_attn(zqkv, row(lambda_q1), row(lambda_k1), row(lambda_q2), row(lambda_k2),
                        row(diff_subln), batch=b, seq=s, tq=256, lam_init=lam_init)
        m = _merge(ya, cv, h, w_attn_out[l], w_conv_out[l], w_in, b_gates[l], tm=1024, tn=256)
        x2 = _mix_out(m, w_mix_out[l], x1, tm=1024, tn=512)

        kv = _mem_kv(memt, row(mem_norm), w_xkv[l], tm=512)
        x3 = _xattn(x2, row(xattn_norm), w_xq[l], kv, w_xo[l], seq=s, tm=512)

        w2 = _cast_pad_ffn_weights(ffn2_w_gate[l], ffn2_w_up[l], ffn2_w_down[l], tf=ffn_tf)
        g_next = final_norm.reshape(1, -1) if last else ffn1_norm[l + 1].reshape(1, -1)
        res = _ffn(x3, row(ffn2_norm), *w2, g_next, emit_y=not last, tm=512, tf=ffn_tf)
        if last:
            out = res
        else:
            xt = res[0]
    return out.reshape(b, s, d)
```

```python
import functools
import math

import jax
import jax.numpy as jnp
from jax import lax
from jax.experimental import pallas as pl
from jax.experimental.pallas import tpu as pltpu

D_MODEL = 2048
MEM_LEN = 256
DIFF_HEADS = 8
DIFF_HEAD_DIM = 128
DIFF_PAIR = 2 * DIFF_HEAD_DIM
CONV_K = 3
XATTN_HEADS = 4
XATTN_HEAD_DIM = 128
XATTN_WIDTH = XATTN_HEADS * XATTN_HEAD_DIM
NORM_EPS = 1e-6
SUBLN_EPS = 1e-5

V7X_LANES = 128
V7X_SUBLANES = 8
V7X_MXU_DIM = 256
V7X_VMEM_LIMIT_BYTES = 56 * 1024 * 1024

BF16 = jnp.bfloat16
F32 = jnp.float32


def _params(n_axes):
    return pltpu.CompilerParams(
        dimension_semantics=("arbitrary",) * n_axes,
        vmem_limit_bytes=V7X_VMEM_LIMIT_BYTES)


def _dot(a, b):
    return jnp.dot(a, b, preferred_element_type=F32)


def _dot_nt(a, b):
    return lax.dot_general(a, b, (((1,), (1,)), ((), ())), preferred_element_type=F32)


def _rms(x, g, eps):
    return x * lax.rsqrt(jnp.mean(x * x, axis=-1, keepdims=True) + eps) * g


def _sigmoid(x):
    return 1.0 / (1.0 + jnp.exp(-x))


def _cast_weights_once(step, pairs):
    @pl.when(step == 0)
    def _():
        for src, dst in pairs:
            dst[...] = src[...].astype(dst.dtype)


def _ffn_kernel(x_ref, g_ref, wg_ref, wu_ref, wd_ref, g2_ref, *refs, emit_y, n_sub, n_f, f):
    if emit_y:
        y_ref, n_ref, h_ref, acc_ref = refs
    else:
        n_ref, h_ref, acc_ref = refs
    s = pl.program_id(1)
    tr = x_ref.shape[0]
    tf = wd_ref.shape[0]

    @pl.when(s < n_sub)
    def _():
        rows = pl.ds(pl.multiple_of(s * tr, tr), tr)
        h_ref[rows, :] = _rms(x_ref[...], g_ref[...], NORM_EPS).astype(BF16)
        acc_ref[rows, :] = jnp.zeros((tr, acc_ref.shape[1]), F32)

    @pl.when((s >= n_sub) & (s < n_sub + n_f))
    def _():
        h = h_ref[...]
        gate = _dot(h, wg_ref[...].astype(BF16))
        up = _dot(h, wu_ref[...].astype(BF16))
        act = gate * _sigmoid(gate) * up
        wd = wd_ref[...]
        if f % tf:
            valid = f - (s - n_sub) * tf
            act = jnp.where(lax.broadcasted_iota(jnp.int32, act.shape, 1) < valid, act, 0.0)
            wd = jnp.where(lax.broadcasted_iota(jnp.int32, wd.shape, 0) < valid, wd, 0.0)
        acc_ref[...] += _dot(act.astype(BF16), wd.astype(BF16))

    @pl.when(s >= n_sub + n_f)
    def _():
        rows = pl.ds(pl.multiple_of((s - n_sub - n_f) * tr, tr), tr)
        y = x_ref[...] + 0.5 * acc_ref[rows, :]
        if emit_y:
            y_ref[...] = y
        n_ref[...] = _rms(y, g2_ref[...], NORM_EPS).astype(n_ref.dtype)


def _ffn(x, g, wg, wu, wd, g2, *, emit_y, tc, tr, tf):
    t, d = x.shape
    f = wg.shape[1]
    n_sub = tc // tr
    n_f = pl.cdiv(f, tf)

    def x_rows(c, s):
        return (c * n_sub + jnp.where(s < n_sub, s, jnp.maximum(s - n_sub - n_f, 0)), 0)

    def out_rows(c, s):
        return (c * n_sub + jnp.maximum(s - n_sub - n_f, 0), 0)

    f_tile = lambda s: jnp.clip(s - n_sub, 0, n_f - 1)
    vec = pl.BlockSpec((1, d), lambda c, s: (0, 0))
    in_specs = [pl.BlockSpec((tr, d), x_rows), vec,
                pl.BlockSpec((d, tf), lambda c, s: (0, f_tile(s))),
                pl.BlockSpec((d, tf), lambda c, s: (0, f_tile(s))),
                pl.BlockSpec((tf, d), lambda c, s: (f_tile(s), 0)),
                vec]
    out_row = pl.BlockSpec((tr, d), out_rows)
    if emit_y:
        out_shape = (jax.ShapeDtypeStruct((t, d), F32), jax.ShapeDtypeStruct((t, d), BF16))
        out_specs = (out_row, out_row)
    else:
        out_shape = jax.ShapeDtypeStruct((t, d), F32)
        out_specs = out_row
    return pl.pallas_call(
        functools.partial(_ffn_kernel, emit_y=emit_y, n_sub=n_sub, n_f=n_f, f=f),
        grid=(t // tc, 2 * n_sub + n_f),
        in_specs=in_specs, out_specs=out_specs, out_shape=out_shape,
        scratch_shapes=[pltpu.VMEM((tc, d), BF16), pltpu.VMEM((tc, d), F32)],
        compiler_params=_params(2),
        name="ffn_y" if emit_y else "ffn_final",
    )(x, g, wg, wu, wd, g2)


def _qkv_kernel(h_ref, w_ref, o_ref, wb_ref, *, q_blocks, scale):
    j = pl.program_id(0)
    _cast_weights_once(pl.program_id(1), [(w_ref, wb_ref)])
    acc = _dot(h_ref[...], wb_ref[...])
    acc = acc * jnp.where(j < q_blocks, scale, 1.0).astype(F32)
    o_ref[...] = acc.astype(o_ref.dtype)


def _qkv(h, w, *, tm, tn):
    t, d = h.shape
    n = 3 * d
    return pl.pallas_call(
        functools.partial(_qkv_kernel, q_blocks=d // tn, scale=DIFF_HEAD_DIM ** -0.5),
        grid=(n // tn, t // tm),
        in_specs=[pl.BlockSpec((tm, d), lambda j, i: (i, 0)),
                  pl.BlockSpec((d, tn), lambda j, i: (0, j))],
        out_specs=pl.BlockSpec((tm, tn), lambda j, i: (i, j)),
        out_shape=jax.ShapeDtypeStruct((t, n), BF16),
        scratch_shapes=[pltpu.VMEM((d, tn), BF16)],
        compiler_params=_params(2),
        name="qkv_proj",
    )(h, w)


def _conv_kernel(h_ref, wb_ref, wc_ref, wu_ref, cw_ref, o_ref, carry_ref, wbb_ref, wcb_ref, wub_ref,
                 *, tiles_per_seq):
    i = pl.program_id(1)
    _cast_weights_once(i, [(wb_ref, wbb_ref), (wc_ref, wcb_ref), (wu_ref, wub_ref)])
    h = h_ref[...]
    cu = _dot(h, wcb_ref[...]) * _dot(h, wub_ref[...])
    tm = cu.shape[0]

    @pl.when(i % tiles_per_seq == 0)
    def _():
        carry_ref[...] = jnp.zeros_like(carry_ref)

    prev = carry_ref[...]
    carry_ref[...] = cu[tm - V7X_SUBLANES:, :]

    row8 = lax.broadcasted_iota(jnp.int32, prev.shape, 0)

    def shifted(s):
        rolled = pltpu.roll(cu, s, axis=0)
        head = jnp.where(row8 < s, pltpu.roll(prev, s, axis=0), rolled[:V7X_SUBLANES, :])
        return jnp.concatenate([head, rolled[V7X_SUBLANES:, :]], axis=0)

    cw = cw_ref[...]
    y = cw[2:3, :] * cu + cw[1:2, :] * shifted(1) + cw[0:1, :] * shifted(2)
    o_ref[...] = (_dot(h, wbb_ref[...]) * y).astype(o_ref.dtype)


def _conv_branch(h, w, conv_w, *, seq, tm, tn):
    t, d = h.shape
    nb = d // tn
    wspec = lambda k: pl.BlockSpec((d, tn), lambda j, i: (0, k * nb + j))
    return pl.pallas_call(
        functools.partial(_conv_kernel, tiles_per_seq=seq // tm),
        grid=(nb, t // tm),
        in_specs=[pl.BlockSpec((tm, d), lambda j, i: (i, 0)),
                  wspec(3), wspec(4), wspec(5),
                  pl.BlockSpec((CONV_K, tn), lambda j, i: (0, j))],
        out_specs=pl.BlockSpec((tm, tn), lambda j, i: (i, j)),
        out_shape=jax.ShapeDtypeStruct((t, d), BF16),
        scratch_shapes=[pltpu.VMEM((V7X_SUBLANES, tn), F32)] + [pltpu.VMEM((d, tn), BF16)] * 3,
        compiler_params=_params(2),
        name="conv_branch",
    )(h, w, w, w, conv_w)


def _diff_attn_kernel(q_ref, k_ref, v_ref, lq1_ref, lk1_ref, lq2_ref, lk2_ref, sg_ref, o_ref,
                      *, tq, nq, lam_init):
    hd = DIFF_HEAD_DIM
    lam = (jnp.exp(jnp.sum(lq1_ref[...] * lk1_ref[...], axis=-1, keepdims=True))
           - jnp.exp(jnp.sum(lq2_ref[...] * lk2_ref[...], axis=-1, keepdims=True))
           + lam_init)
    sg = sg_ref[...]
    keep = (lax.broadcasted_iota(jnp.int32, (tq, tq), 1)
            <= lax.broadcasted_iota(jnp.int32, (tq, tq), 0))

    def softmax_terms(qc, kc, r0):
        s = _dot_nt(qc, kc)
        diag = jnp.where(keep, s[:, r0:], -jnp.inf)
        s = jnp.concatenate([s[:, :r0], diag], axis=1) if r0 else diag
        p = jnp.exp(s - jnp.max(s, axis=-1, keepdims=True))
        return p, jnp.sum(p, axis=-1, keepdims=True)

    for qi in range(nq):
        r0 = qi * tq
        q = q_ref[r0:r0 + tq, :]
        k = k_ref[0:r0 + tq, :]
        p1, l1 = softmax_terms(q[:, :hd], k[:, :hd], r0)
        p2, l2 = softmax_terms(q[:, hd:], k[:, hd:], r0)
        a = p1 * (1.0 / l1) - p2 * (lam / l2)
        o = _dot(a.astype(BF16), v_ref[0:r0 + tq, :])
        o = _rms(o, sg, SUBLN_EPS) * (1.0 - lam_init)
        o_ref[r0:r0 + tq, :] = o.astype(o_ref.dtype)


def _diff_attn(zqkv, lq1, lk1, lq2, lk2, subln, *, batch, seq, tq, lam_init):
    t = zqkv.shape[0]
    w = DIFF_PAIR
    lvec = pl.BlockSpec((1, DIFF_HEAD_DIM), lambda b, h: (0, 0))
    return pl.pallas_call(
        functools.partial(_diff_attn_kernel, tq=tq, nq=seq // tq, lam_init=lam_init),
        grid=(batch, DIFF_HEADS),
        in_specs=[pl.BlockSpec((seq, w), lambda b, h: (b, h)),
                  pl.BlockSpec((seq, w), lambda b, h: (b, DIFF_HEADS + h)),
                  pl.BlockSpec((seq, w), lambda b, h: (b, 2 * DIFF_HEADS + h)),
                  lvec, lvec, lvec, lvec,
                  pl.BlockSpec((1, w), lambda b, h: (0, 0))],
        out_specs=pl.BlockSpec((seq, w), lambda b, h: (b, h)),
        out_shape=jax.ShapeDtypeStruct((t, DIFF_HEADS * w), BF16),
        compiler_params=_params(2),
        name="diff_attn",
    )(zqkv, zqkv, zqkv, lq1, lk1, lq2, lk2, subln)


def _merge_kernel(ya_ref, cv_ref, h_ref, wa_ref, wc_ref, wga_ref, wgc_ref, b_ref, o_ref):
    h = h_ref[...]
    b = b_ref[...]
    ga = _sigmoid(_dot(h, wga_ref[...].astype(BF16)) + b[0:1, :])
    gc = _sigmoid(_dot(h, wgc_ref[...].astype(BF16)) + b[1:2, :])
    m = (ga * _dot(ya_ref[...], wa_ref[...].astype(BF16))
         + gc * _dot(cv_ref[...], wc_ref[...].astype(BF16)))
    o_ref[...] = m.astype(o_ref.dtype)


def _merge(ya, cv, h, wa, wc, w_mix_in, b_gates, *, tm, tn):
    t, d = h.shape
    nb = d // tn
    row = pl.BlockSpec((tm, d), lambda i, j: (i, 0))
    col = pl.BlockSpec((d, tn), lambda i, j: (0, j))
    return pl.pallas_call(
        _merge_kernel,
        grid=(t // tm, nb),
        in_specs=[row, row, row, col, col,
                  pl.BlockSpec((d, tn), lambda i, j: (0, 6 * nb + j)),
                  pl.BlockSpec((d, tn), lambda i, j: (0, 7 * nb + j)),
                  pl.BlockSpec((2, tn), lambda i, j: (0, j))],
        out_specs=pl.BlockSpec((tm, tn), lambda i, j: (i, j)),
        out_shape=jax.ShapeDtypeStruct((t, d), BF16),
        compiler_params=_params(2),
        name="gated_merge",
    )(ya, cv, h, wa, wc, w_mix_in, w_mix_in, b_gates)


def _mix_out_kernel(m_ref, w_ref, x_ref, y_ref, wb_ref):
    _cast_weights_once(pl.program_id(0), [(w_ref, wb_ref)])
    y_ref[...] = x_ref[...] + _dot(m_ref[...], wb_ref[...])


def _mix_out(m, w, x, *, tm):
    t, d = x.shape
    row = pl.BlockSpec((tm, d), lambda i: (i, 0))
    return pl.pallas_call(
        _mix_out_kernel,
        grid=(t // tm,),
        in_specs=[row,
                  pl.BlockSpec((d, d), lambda i: (0, 0), pipeline_mode=pl.Buffered(1)),
                  row],
        out_specs=row,
        out_shape=jax.ShapeDtypeStruct((t, d), F32),
        scratch_shapes=[pltpu.VMEM((d, d), BF16)],
        compiler_params=_params(1),
        name="mix_out",
    )(m, w, x)


def _mem_kv_kernel(m_ref, g_ref, w_ref, o_ref, wb_ref):
    _cast_weights_once(pl.program_id(0), [(w_ref, wb_ref)])
    hm = _rms(m_ref[...], g_ref[...], NORM_EPS).astype(BF16)
    o_ref[...] = _dot(hm, wb_ref[...]).astype(o_ref.dtype)


def _mem_kv(mem, g, w, *, tm):
    t, d = mem.shape
    n = w.shape[1]
    return pl.pallas_call(
        _mem_kv_kernel,
        grid=(t // tm,),
        in_specs=[pl.BlockSpec((tm, d), lambda i: (i, 0)),
                  pl.BlockSpec((1, d), lambda i: (0, 0)),
                  pl.BlockSpec((d, n), lambda i: (0, 0))],
        out_specs=pl.BlockSpec((tm, n), lambda i: (i, 0)),
        out_shape=jax.ShapeDtypeStruct((t, n), BF16),
        scratch_shapes=[pltpu.VMEM((d, n), BF16)],
        compiler_params=_params(1),
        name="mem_kv",
    )(mem, g, w)


def _xattn_kernel(x_ref, g_ref, wq_ref, kv_ref, wo_ref, y_ref, wqb_ref, wob_ref):
    _cast_weights_once(pl.program_id(0), [(wq_ref, wqb_ref), (wo_ref, wob_ref)])
    hd = XATTN_HEAD_DIM
    x = x_ref[...]
    n = _rms(x, g_ref[...], NORM_EPS).astype(BF16)
    q = (_dot(n, wqb_ref[...]) * (hd ** -0.5)).astype(BF16)
    kv = kv_ref[...]
    outs = []
    for h in range(XATTN_HEADS):
        k = kv[:, h * hd:(h + 1) * hd]
        v = kv[:, XATTN_WIDTH + h * hd:XATTN_WIDTH + (h + 1) * hd]
        s = _dot_nt(q[:, h * hd:(h + 1) * hd], k)
        p = jnp.exp(s - jnp.max(s, axis=-1, keepdims=True))
        l = jnp.sum(p, axis=-1, keepdims=True)
        outs.append((_dot(p.astype(BF16), v) / l).astype(BF16))
    o = jnp.concatenate(outs, axis=-1)
    y_ref[...] = x + _dot(o, wob_ref[...])


def _xattn(x, g, wq, kv, wo, *, seq, tm):
    t, d = x.shape
    tiles_per_seq = seq // tm
    row = pl.BlockSpec((tm, d), lambda i: (i, 0))
    return pl.pallas_call(
        _xattn_kernel,
        grid=(t // tm,),
        in_specs=[row,
                  pl.BlockSpec((1, d), lambda i: (0, 0)),
                  pl.BlockSpec((d, XATTN_WIDTH), lambda i: (0, 0)),
                  pl.BlockSpec((MEM_LEN, 2 * XATTN_WIDTH), lambda i: (i // tiles_per_seq, 0)),
                  pl.BlockSpec((XATTN_WIDTH, d), lambda i: (0, 0))],
        out_specs=row,
        out_shape=jax.ShapeDtypeStruct((t, d), F32),
        scratch_shapes=[pltpu.VMEM((d, XATTN_WIDTH), BF16), pltpu.VMEM((XATTN_WIDTH, d), BF16)],
        compiler_params=_params(1),
        name="xattn",
    )(x, g, wq, kv, wo)


def kernel(x, mem, ffn1_norm, ffn1_w_gate, ffn1_w_up, ffn1_w_down, mix_norm, w_mix_in, b_gates,
           lambda_q1, lambda_k1, lambda_q2, lambda_k2, diff_subln, w_attn_out, conv_w, w_conv_out,
           w_mix_out, xattn_norm, mem_norm, w_xq, w_xkv, w_xo, ffn2_norm, ffn2_w_gate, ffn2_w_up,
           ffn2_w_down, final_norm):
    b, s, d = x.shape
    depth = ffn1_norm.shape[0]
    t = b * s
    ffn_tiles = dict(tc=2048, tr=256, tf=V7X_MXU_DIM)

    xt = x.reshape(t, d)
    memt = mem.reshape(b * mem.shape[1], d)
    out = None
    for l in range(depth):
        last = l == depth - 1
        row = lambda a: a[l].reshape(1, -1)
        lam_init = 0.8 - 0.6 * math.exp(-0.3 * l)

        x1, h = _ffn(xt, row(ffn1_norm), ffn1_w_gate[l], ffn1_w_up[l], ffn1_w_down[l],
                     row(mix_norm), emit_y=True, **ffn_tiles)

        w_in = w_mix_in[l]
        zqkv = _qkv(h, w_in, tm=1024, tn=1024)
        cv = _conv_branch(h, w_in, conv_w[l], seq=s, tm=512, tn=512)
        ya = _diff_attn(zqkv, row(lambda_q1), row(lambda_k1), row(lambda_q2), row(lambda_k2),
                        row(diff_subln), batch=b, seq=s, tq=256, lam_init=lam_init)
        m = _merge(ya, cv, h, w_attn_out[l], w_conv_out[l], w_in, b_gates[l], tm=1024, tn=256)
        x2 = _mix_out(m, w_mix_out[l], x1, tm=512)

        kv = _mem_kv(memt, row(mem_norm), w_xkv[l], tm=512)
        x3 = _xattn(x2, row(xattn_norm), w_xq[l], kv, w_xo[l], seq=s, tm=512)

        g_next = final_norm.reshape(1, -1) if last else ffn1_norm[l + 1].reshape(1, -1)
        res = _ffn(x3, row(ffn2_norm), ffn2_w_gate[l], ffn2_w_up[l], ffn2_w_down[l],
                   g_next, emit_y=not last, **ffn_tiles)
        if last:
            out = res
        else:
            xt = res[0]
    return out.reshape(b, s, d)
```

```python
import functools
import math

import jax
import jax.numpy as jnp
from jax import lax
from jax.experimental import pallas as pl
from jax.experimental.pallas import tpu as pltpu

D_MODEL = 2048
MEM_LEN = 256
DIFF_HEADS = 8
DIFF_HEAD_DIM = 128
DIFF_PAIR = 2 * DIFF_HEAD_DIM
CONV_K = 3
XATTN_HEADS = 4
XATTN_HEAD_DIM = 128
XATTN_WIDTH = XATTN_HEADS * XATTN_HEAD_DIM
NORM_EPS = 1e-6
SUBLN_EPS = 1e-5
LOG2_E = math.log2(math.e)

V7X_LANES = 128
V7X_SUBLANES = 8
V7X_MXU_DIM = 256
V7X_VMEM_LIMIT_BYTES = 56 * 1024 * 1024

BF16 = jnp.bfloat16
F32 = jnp.float32


def _params(n_axes):
    return pltpu.CompilerParams(
        dimension_semantics=("arbitrary",) * n_axes,
        vmem_limit_bytes=V7X_VMEM_LIMIT_BYTES)


def _dot(a, b):
    return jnp.dot(a, b, preferred_element_type=F32)


def _dot_nt(a, b):
    return lax.dot_general(a, b, (((1,), (1,)), ((), ())), preferred_element_type=F32)


def _rms(x, g, eps):
    return x * lax.rsqrt(jnp.mean(x * x, axis=-1, keepdims=True) + eps) * g


def _sigmoid(x):
    return 1.0 / (1.0 + jnp.exp(-x))


def _cast_weights_once(step, pairs):
    @pl.when(step == 0)
    def _():
        for src, dst in pairs:
            dst[...] = src[...].astype(dst.dtype)


def _ffn_kernel(x_ref, g_ref, wg_ref, wu_ref, wd_ref, g2_ref, *refs, emit_y, n_sub, n_f, f):
    if emit_y:
        y_ref, n_ref, h_ref, acc_ref = refs
    else:
        n_ref, h_ref, acc_ref = refs
    s = pl.program_id(1)
    tr = x_ref.shape[0]
    tf = wd_ref.shape[0]

    @pl.when(s < n_sub)
    def _():
        rows = pl.ds(pl.multiple_of(s * tr, tr), tr)
        x = x_ref[...]
        h_ref[rows, :] = _rms(x, g_ref[...], NORM_EPS).astype(BF16)
        acc_ref[rows, :] = 2.0 * x

    @pl.when((s >= n_sub) & (s < n_sub + n_f))
    def _():
        h = h_ref[...]
        gate = _dot(h, wg_ref[...].astype(BF16))
        up = _dot(h, wu_ref[...].astype(BF16))
        act = gate * _sigmoid(gate) * up
        wd = wd_ref[...]
        if f % tf:
            valid = f - (s - n_sub) * tf
            act = jnp.where(lax.broadcasted_iota(jnp.int32, act.shape, 1) < valid, act, 0.0)
            wd = jnp.where(lax.broadcasted_iota(jnp.int32, wd.shape, 0) < valid, wd, 0.0)
        acc_ref[...] += _dot(act.astype(BF16), wd.astype(BF16))

    @pl.when(s >= n_sub + n_f)
    def _():
        rows = pl.ds(pl.multiple_of((s - n_sub - n_f) * tr, tr), tr)
        y = 0.5 * acc_ref[rows, :]
        if emit_y:
            y_ref[...] = y
        n_ref[...] = _rms(y, g2_ref[...], NORM_EPS).astype(n_ref.dtype)


def _ffn(x, g, wg, wu, wd, g2, *, emit_y, tc, tr, tf):
    t, d = x.shape
    f = wg.shape[1]
    n_sub = tc // tr
    n_f = pl.cdiv(f, tf)

    def x_rows(c, s):
        return (c * n_sub + jnp.minimum(s, n_sub - 1), 0)

    def out_rows(c, s):
        return (c * n_sub + jnp.maximum(s - n_sub - n_f, 0), 0)

    f_tile = lambda s: jnp.clip(s - n_sub, 0, n_f - 1)
    vec = pl.BlockSpec((1, d), lambda c, s: (0, 0))
    in_specs = [pl.BlockSpec((tr, d), x_rows), vec,
                pl.BlockSpec((d, tf), lambda c, s: (0, f_tile(s))),
                pl.BlockSpec((d, tf), lambda c, s: (0, f_tile(s))),
                pl.BlockSpec((tf, d), lambda c, s: (f_tile(s), 0)),
                vec]
    out_row = pl.BlockSpec((tr, d), out_rows)
    if emit_y:
        out_shape = (jax.ShapeDtypeStruct((t, d), F32), jax.ShapeDtypeStruct((t, d), BF16))
        out_specs = (out_row, out_row)
    else:
        out_shape = jax.ShapeDtypeStruct((t, d), F32)
        out_specs = out_row
    return pl.pallas_call(
        functools.partial(_ffn_kernel, emit_y=emit_y, n_sub=n_sub, n_f=n_f, f=f),
        grid=(t // tc, 2 * n_sub + n_f),
        in_specs=in_specs, out_specs=out_specs, out_shape=out_shape,
        scratch_shapes=[pltpu.VMEM((tc, d), BF16), pltpu.VMEM((tc, d), F32)],
        compiler_params=_params(2),
        name="ffn_y" if emit_y else "ffn_final",
    )(x, g, wg, wu, wd, g2)


def _qkv_kernel(h_ref, w_ref, o_ref, wb_ref, *, q_blocks, scale):
    j = pl.program_id(0)
    _cast_weights_once(pl.program_id(1), [(w_ref, wb_ref)])
    acc = _dot(h_ref[...], wb_ref[...])
    acc = acc * jnp.where(j < q_blocks, scale, 1.0).astype(F32)
    o_ref[...] = acc.astype(o_ref.dtype)


def _qkv(h, w, *, tm, tn):
    t, d = h.shape
    n = 3 * d
    return pl.pallas_call(
        functools.partial(_qkv_kernel, q_blocks=d // tn, scale=DIFF_HEAD_DIM ** -0.5 * LOG2_E),
        grid=(n // tn, t // tm),
        in_specs=[pl.BlockSpec((tm, d), lambda j, i: (i, 0)),
                  pl.BlockSpec((d, tn), lambda j, i: (0, j))],
        out_specs=pl.BlockSpec((tm, tn), lambda j, i: (i, j)),
        out_shape=jax.ShapeDtypeStruct((t, n), BF16),
        scratch_shapes=[pltpu.VMEM((d, tn), BF16)],
        compiler_params=_params(2),
        name="qkv_proj",
    )(h, w)


def _conv_kernel(h_ref, wb_ref, wc_ref, wu_ref, cw_ref, o_ref, carry_ref, wbb_ref, wcb_ref, wub_ref,
                 *, tiles_per_seq):
    i = pl.program_id(1)
    _cast_weights_once(i, [(wb_ref, wbb_ref), (wc_ref, wcb_ref), (wu_ref, wub_ref)])
    h = h_ref[...]
    cu = _dot(h, wcb_ref[...]) * _dot(h, wub_ref[...])
    tm = cu.shape[0]

    @pl.when(i % tiles_per_seq == 0)
    def _():
        carry_ref[...] = jnp.zeros_like(carry_ref)

    prev = carry_ref[...]
    carry_ref[...] = cu[tm - V7X_SUBLANES:, :]

    row8 = lax.broadcasted_iota(jnp.int32, prev.shape, 0)

    def shifted(s):
        rolled = pltpu.roll(cu, s, axis=0)
        head = jnp.where(row8 < s, pltpu.roll(prev, s, axis=0), rolled[:V7X_SUBLANES, :])
        return jnp.concatenate([head, rolled[V7X_SUBLANES:, :]], axis=0)

    cw = cw_ref[...]
    y = cw[2:3, :] * cu + cw[1:2, :] * shifted(1) + cw[0:1, :] * shifted(2)
    o_ref[...] = (_dot(h, wbb_ref[...]) * y).astype(o_ref.dtype)


def _conv_branch(h, w, conv_w, *, seq, tm, tn):
    t, d = h.shape
    nb = d // tn
    wspec = lambda k: pl.BlockSpec((d, tn), lambda j, i: (0, k * nb + j))
    return pl.pallas_call(
        functools.partial(_conv_kernel, tiles_per_seq=seq // tm),
        grid=(nb, t // tm),
        in_specs=[pl.BlockSpec((tm, d), lambda j, i: (i, 0)),
                  wspec(3), wspec(4), wspec(5),
                  pl.BlockSpec((CONV_K, tn), lambda j, i: (0, j))],
        out_specs=pl.BlockSpec((tm, tn), lambda j, i: (i, j)),
        out_shape=jax.ShapeDtypeStruct((t, d), BF16),
        scratch_shapes=[pltpu.VMEM((V7X_SUBLANES, tn), F32)] + [pltpu.VMEM((d, tn), BF16)] * 3,
        compiler_params=_params(2),
        name="conv_branch",
    )(h, w, w, w, conv_w)


def _diff_attn_kernel(q_ref, k_ref, v_ref, lq1_ref, lk1_ref, lq2_ref, lk2_ref, sg_ref, o_ref,
                      *, tq, nq, lam_init):
    hd = DIFF_HEAD_DIM
    lam = (jnp.exp(jnp.sum(lq1_ref[...] * lk1_ref[...], axis=-1, keepdims=True))
           - jnp.exp(jnp.sum(lq2_ref[...] * lk2_ref[...], axis=-1, keepdims=True))
           + lam_init)
    sg = sg_ref[...]
    keep = (lax.broadcasted_iota(jnp.int32, (tq, tq), 1)
            <= lax.broadcasted_iota(jnp.int32, (tq, tq), 0))

    def softmax_terms(qc, kc, r0):
        s = _dot_nt(qc, kc)
        diag = jnp.where(keep, s[:, r0:], -jnp.inf)
        s = jnp.concatenate([s[:, :r0], diag], axis=1) if r0 else diag
        p = jnp.exp2(s - jnp.max(s, axis=-1, keepdims=True))
        return p, jnp.sum(p, axis=-1, keepdims=True)

    for qi in range(nq):
        r0 = qi * tq
        q = q_ref[r0:r0 + tq, :]
        k = k_ref[0:r0 + tq, :]
        p1, l1 = softmax_terms(q[:, :hd], k[:, :hd], r0)
        p2, l2 = softmax_terms(q[:, hd:], k[:, hd:], r0)
        a = p1 - p2 * (lam * l1 / l2)
        o = _dot(a.astype(BF16), v_ref[0:r0 + tq, :]) / l1
        o = _rms(o, sg, SUBLN_EPS) * (1.0 - lam_init)
        o_ref[r0:r0 + tq, :] = o.astype(o_ref.dtype)


def _diff_attn(zqkv, lq1, lk1, lq2, lk2, subln, *, batch, seq, tq, lam_init):
    t = zqkv.shape[0]
    w = DIFF_PAIR
    lvec = pl.BlockSpec((1, DIFF_HEAD_DIM), lambda b, h: (0, 0))
    return pl.pallas_call(
        functools.partial(_diff_attn_kernel, tq=tq, nq=seq // tq, lam_init=lam_init),
        grid=(batch, DIFF_HEADS),
        in_specs=[pl.BlockSpec((seq, w), lambda b, h: (b, h)),
                  pl.BlockSpec((seq, w), lambda b, h: (b, DIFF_HEADS + h)),
                  pl.BlockSpec((seq, w), lambda b, h: (b, 2 * DIFF_HEADS + h)),
                  lvec, lvec, lvec, lvec,
                  pl.BlockSpec((1, w), lambda b, h: (0, 0))],
        out_specs=pl.BlockSpec((seq, w), lambda b, h: (b, h)),
        out_shape=jax.ShapeDtypeStruct((t, DIFF_HEADS * w), BF16),
        compiler_params=_params(2),
        name="diff_attn",
    )(zqkv, zqkv, zqkv, lq1, lk1, lq2, lk2, subln)


def _merge_kernel(ya_ref, cv_ref, h_ref, wa_ref, wc_ref, wga_ref, wgc_ref, b_ref, o_ref):
    h = h_ref[...]
    b = b_ref[...]
    ga = _sigmoid(_dot(h, wga_ref[...].astype(BF16)) + b[0:1, :])
    gc = _sigmoid(_dot(h, wgc_ref[...].astype(BF16)) + b[1:2, :])
    m = (ga * _dot(ya_ref[...], wa_ref[...].astype(BF16))
         + gc * _dot(cv_ref[...], wc_ref[...].astype(BF16)))
    o_ref[...] = m.astype(o_ref.dtype)


def _merge(ya, cv, h, wa, wc, w_mix_in, b_gates, *, tm, tn):
    t, d = h.shape
    nb = d // tn
    row = pl.BlockSpec((tm, d), lambda i, j: (i, 0))
    col = pl.BlockSpec((d, tn), lambda i, j: (0, j))
    return pl.pallas_call(
        _merge_kernel,
        grid=(t // tm, nb),
        in_specs=[row, row, row, col, col,
                  pl.BlockSpec((d, tn), lambda i, j: (0, 6 * nb + j)),
                  pl.BlockSpec((d, tn), lambda i, j: (0, 7 * nb + j)),
                  pl.BlockSpec((2, tn), lambda i, j: (0, j))],
        out_specs=pl.BlockSpec((tm, tn), lambda i, j: (i, j)),
        out_shape=jax.ShapeDtypeStruct((t, d), BF16),
        compiler_params=_params(2),
        name="gated_merge",
    )(ya, cv, h, wa, wc, w_mix_in, w_mix_in, b_gates)


def _mix_out_kernel(m_ref, w_ref, x_ref, y_ref, wb_ref):
    _cast_weights_once(pl.program_id(0), [(w_ref, wb_ref)])
    y_ref[...] = x_ref[...] + _dot(m_ref[...], wb_ref[...])


def _mix_out(m, w, x, *, tm):
    t, d = x.shape
    row = pl.BlockSpec((tm, d), lambda i: (i, 0))
    return pl.pallas_call(
        _mix_out_kernel,
        grid=(t // tm,),
        in_specs=[row,
                  pl.BlockSpec((d, d), lambda i: (0, 0), pipeline_mode=pl.Buffered(1)),
                  row],
        out_specs=row,
        out_shape=jax.ShapeDtypeStruct((t, d), F32),
        scratch_shapes=[pltpu.VMEM((d, d), BF16)],
        compiler_params=_params(1),
        name="mix_out",
    )(m, w, x)


def _mem_kv_kernel(m_ref, g_ref, w_ref, o_ref, wb_ref):
    _cast_weights_once(pl.program_id(0), [(w_ref, wb_ref)])
    hm = _rms(m_ref[...], g_ref[...], NORM_EPS).astype(BF16)
    o_ref[...] = _dot(hm, wb_ref[...]).astype(o_ref.dtype)


def _mem_kv(mem, g, w, *, tm):
    t, d = mem.shape
    n = w.shape[1]
    return pl.pallas_call(
        _mem_kv_kernel,
        grid=(t // tm,),
        in_specs=[pl.BlockSpec((tm, d), lambda i: (i, 0)),
                  pl.BlockSpec((1, d), lambda i: (0, 0)),
                  pl.BlockSpec((d, n), lambda i: (0, 0))],
        out_specs=pl.BlockSpec((tm, n), lambda i: (i, 0)),
        out_shape=jax.ShapeDtypeStruct((t, n), BF16),
        scratch_shapes=[pltpu.VMEM((d, n), BF16)],
        compiler_params=_params(1),
        name="mem_kv",
    )(mem, g, w)


def _xattn_kernel(x_ref, g_ref, wq_ref, kv_ref, wo_ref, y_ref, wqb_ref, wob_ref):
    _cast_weights_once(pl.program_id(0), [(wq_ref, wqb_ref), (wo_ref, wob_ref)])
    hd = XATTN_HEAD_DIM
    x = x_ref[...]
    n = _rms(x, g_ref[...], NORM_EPS).astype(BF16)
    q = (_dot(n, wqb_ref[...]) * (hd ** -0.5)).astype(BF16)
    kv = kv_ref[...]
    outs = []
    for h in range(XATTN_HEADS):
        k = kv[:, h * hd:(h + 1) * hd]
        v = kv[:, XATTN_WIDTH + h * hd:XATTN_WIDTH + (h + 1) * hd]
        s = _dot_nt(q[:, h * hd:(h + 1) * hd], k)
        p = jnp.exp(s - jnp.max(s, axis=-1, keepdims=True))
        l = jnp.sum(p, axis=-1, keepdims=True)
        outs.append((_dot(p.astype(BF16), v) / l).astype(BF16))
    o = jnp.concatenate(outs, axis=-1)
    y_ref[...] = x + _dot(o, wob_ref[...])


def _xattn(x, g, wq, kv, wo, *, seq, tm):
    t, d = x.shape
    tiles_per_seq = seq // tm
    row = pl.BlockSpec((tm, d), lambda i: (i, 0))
    return pl.pallas_call(
        _xattn_kernel,
        grid=(t // tm,),
        in_specs=[row,
                  pl.BlockSpec((1, d), lambda i: (0, 0)),
                  pl.BlockSpec((d, XATTN_WIDTH), lambda i: (0, 0)),
                  pl.BlockSpec((MEM_LEN, 2 * XATTN_WIDTH), lambda i: (i // tiles_per_seq, 0)),
                  pl.BlockSpec((XATTN_WIDTH, d), lambda i: (0, 0))],
        out_specs=row,
        out_shape=jax.ShapeDtypeStruct((t, d), F32),
        scratch_shapes=[pltpu.VMEM((d, XATTN_WIDTH), BF16), pltpu.VMEM((XATTN_WIDTH, d), BF16)],
        compiler_params=_params(1),
        name="xattn",
    )(x, g, wq, kv, wo)


def kernel(x, mem, ffn1_norm, ffn1_w_gate, ffn1_w_up, ffn1_w_down, mix_norm, w_mix_in, b_gates,
           lambda_q1, lambda_k1, lambda_q2, lambda_k2, diff_subln, w_attn_out, conv_w, w_conv_out,
           w_mix_out, xattn_norm, mem_norm, w_xq, w_xkv, w_xo, ffn2_norm, ffn2_w_gate, ffn2_w_up,
           ffn2_w_down, final_norm):
    b, s, d = x.shape
    depth = ffn1_norm.shape[0]
    t = b * s
    ffn_tiles = dict(tc=2048, tr=256, tf=V7X_MXU_DIM)

    xt = x.reshape(t, d)
    memt = mem.reshape(b * mem.shape[1], d)
    out = None
    for l in range(depth):
        last = l == depth - 1
        row = lambda a: a[l].reshape(1, -1)
        lam_init = 0.8 - 0.6 * math.exp(-0.3 * l)

        x1, h = _ffn(xt, row(ffn1_norm), ffn1_w_gate[l], ffn1_w_up[l], ffn1_w_down[l],
                     row(mix_norm), emit_y=True, **ffn_tiles)

        w_in = w_mix_in[l]
        zqkv = _qkv(h, w_in, tm=2048, tn=1024)
        cv = _conv_branch(h, w_in, conv_w[l], seq=s, tm=1024, tn=512)
        ya = _diff_attn(zqkv, row(lambda_q1), row(lambda_k1), row(lambda_q2), row(lambda_k2),
                        row(diff_subln), batch=b, seq=s, tq=256, lam_init=lam_init)
        m = _merge(ya, cv, h, w_attn_out[l], w_conv_out[l], w_in, b_gates[l], tm=1024, tn=256)
        x2 = _mix_out(m, w_mix_out[l], x1, tm=512)

        kv = _mem_kv(memt, row(mem_norm), w_xkv[l], tm=512)
        x3 = _xattn(x2, row(xattn_norm), w_xq[l], kv, w_xo[l], seq=s, tm=512)

        g_next = final_norm.reshape(1, -1) if last else ffn1_norm[l + 1].reshape(1, -1)
        res = _ffn(x3, row(ffn2_norm), ffn2_w_gate[l], ffn2_w_up[l], ffn2_w_down[l],
                   g_next, emit_y=not last, **ffn_tiles)
        if last:
            out = res
        else:
            xt = res[0]
    return out.reshape(b, s, d)
```

```python
import functools
import math

import jax
import jax.numpy as jnp
from jax import lax
from jax.experimental import pallas as pl
from jax.experimental.pallas import tpu as pltpu

D_MODEL = 2048
MEM_LEN = 256
DIFF_HEADS = 8
DIFF_HEAD_DIM = 128
DIFF_PAIR = 2 * DIFF_HEAD_DIM
CONV_K = 3
XATTN_HEADS = 4
XATTN_HEAD_DIM = 128
XATTN_WIDTH = XATTN_HEADS * XATTN_HEAD_DIM
NORM_EPS = 1e-6
SUBLN_EPS = 1e-5
LOG2_E = math.log2(math.e)

V7X_LANES = 128
V7X_SUBLANES = 8
V7X_MXU_DIM = 256
V7X_VMEM_LIMIT_BYTES = 56 * 1024 * 1024

BF16 = jnp.bfloat16
F32 = jnp.float32


def _params(n_axes):
    return pltpu.CompilerParams(
        dimension_semantics=("arbitrary",) * n_axes,
        vmem_limit_bytes=V7X_VMEM_LIMIT_BYTES)


def _dot(a, b):
    return jnp.dot(a, b, preferred_element_type=F32)


def _dot_nt(a, b):
    return lax.dot_general(a, b, (((1,), (1,)), ((), ())), preferred_element_type=F32)


def _rms(x, g, eps):
    return x * lax.rsqrt(jnp.mean(x * x, axis=-1, keepdims=True) + eps) * g


def _sigmoid(x):
    return 1.0 / (1.0 + jnp.exp(-x))


def _cast_weights_once(step, pairs):
    @pl.when(step == 0)
    def _():
        for src, dst in pairs:
            dst[...] = src[...].astype(dst.dtype)


def _ffn_kernel(x_ref, g_ref, wg_ref, wu_ref, wd_ref, g2_ref, *refs, emit_y, n_sub, n_f, f, tm):
    if emit_y:
        y_ref, n_ref, h_ref, acc_ref = refs
    else:
        n_ref, h_ref, acc_ref = refs
    s = pl.program_id(1)
    tr = x_ref.shape[0]
    tf = wd_ref.shape[0]

    @pl.when(s < n_sub)
    def _():
        rows = pl.ds(pl.multiple_of(s * tr, tr), tr)
        x = x_ref[...]
        h_ref[rows, :] = _rms(x, g_ref[...], NORM_EPS).astype(BF16)
        acc_ref[rows, :] = 2.0 * x

    @pl.when((s >= n_sub) & (s < n_sub + n_f))
    def _():
        wg = wg_ref[...].astype(BF16)
        wu = wu_ref[...].astype(BF16)
        wd = wd_ref[...]
        valid = f - (s - n_sub) * tf
        if f % tf:
            wd = jnp.where(lax.broadcasted_iota(jnp.int32, wd.shape, 0) < valid, wd, 0.0)
        wd = wd.astype(BF16)
        for r0 in range(0, h_ref.shape[0], tm):
            h = h_ref[r0:r0 + tm, :]
            gate = _dot(h, wg)
            up = _dot(h, wu)
            act = gate * _sigmoid(gate) * up
            if f % tf:
                act = jnp.where(lax.broadcasted_iota(jnp.int32, act.shape, 1) < valid, act, 0.0)
            acc_ref[r0:r0 + tm, :] += _dot(act.astype(BF16), wd)

    @pl.when(s >= n_sub + n_f)
    def _():
        rows = pl.ds(pl.multiple_of((s - n_sub - n_f) * tr, tr), tr)
        y = 0.5 * acc_ref[rows, :]
        if emit_y:
            y_ref[...] = y
        n_ref[...] = _rms(y, g2_ref[...], NORM_EPS).astype(n_ref.dtype)


def _ffn(x, g, wg, wu, wd, g2, *, emit_y, tc, tr, tf, tm):
    t, d = x.shape
    f = wg.shape[1]
    n_sub = tc // tr
    n_f = pl.cdiv(f, tf)

    def x_rows(c, s):
        return (c * n_sub + jnp.minimum(s, n_sub - 1), 0)

    def out_rows(c, s):
        return (c * n_sub + jnp.maximum(s - n_sub - n_f, 0), 0)

    f_tile = lambda s: jnp.clip(s - n_sub, 0, n_f - 1)
    vec = pl.BlockSpec((1, d), lambda c, s: (0, 0))
    in_specs = [pl.BlockSpec((tr, d), x_rows), vec,
                pl.BlockSpec((d, tf), lambda c, s: (0, f_tile(s))),
                pl.BlockSpec((d, tf), lambda c, s: (0, f_tile(s))),
                pl.BlockSpec((tf, d), lambda c, s: (f_tile(s), 0)),
                vec]
    out_row = pl.BlockSpec((tr, d), out_rows)
    if emit_y:
        out_shape = (jax.ShapeDtypeStruct((t, d), F32), jax.ShapeDtypeStruct((t, d), BF16))
        out_specs = (out_row, out_row)
    else:
        out_shape = jax.ShapeDtypeStruct((t, d), F32)
        out_specs = out_row
    return pl.pallas_call(
        functools.partial(_ffn_kernel, emit_y=emit_y, n_sub=n_sub, n_f=n_f, f=f, tm=tm),
        grid=(t // tc, 2 * n_sub + n_f),
        in_specs=in_specs, out_specs=out_specs, out_shape=out_shape,
        scratch_shapes=[pltpu.VMEM((tc, d), BF16), pltpu.VMEM((tc, d), F32)],
        compiler_params=_params(2),
        name="ffn_y" if emit_y else "ffn_final",
    )(x, g, wg, wu, wd, g2)


def _qkv_kernel(h_ref, w_ref, o_ref, wb_ref, *, q_blocks, scale):
    j = pl.program_id(0)
    _cast_weights_once(pl.program_id(1), [(w_ref, wb_ref)])
    acc = _dot(h_ref[...], wb_ref[...])
    acc = acc * jnp.where(j < q_blocks, scale, 1.0).astype(F32)
    o_ref[...] = acc.astype(o_ref.dtype)


def _qkv(h, w, *, tm, tn):
    t, d = h.shape
    n = 3 * d
    return pl.pallas_call(
        functools.partial(_qkv_kernel, q_blocks=d // tn, scale=DIFF_HEAD_DIM ** -0.5 * LOG2_E),
        grid=(n // tn, t // tm),
        in_specs=[pl.BlockSpec((tm, d), lambda j, i: (i, 0)),
                  pl.BlockSpec((d, tn), lambda j, i: (0, j))],
        out_specs=pl.BlockSpec((tm, tn), lambda j, i: (i, j)),
        out_shape=jax.ShapeDtypeStruct((t, n), BF16),
        scratch_shapes=[pltpu.VMEM((d, tn), BF16)],
        compiler_params=_params(2),
        name="qkv_proj",
    )(h, w)


def _conv_kernel(h_ref, wb_ref, wc_ref, wu_ref, cw_ref, o_ref, carry_ref, wbb_ref, wcb_ref, wub_ref,
                 *, tiles_per_seq):
    i = pl.program_id(1)
    _cast_weights_once(i, [(wb_ref, wbb_ref), (wc_ref, wcb_ref), (wu_ref, wub_ref)])
    h = h_ref[...]
    cu = _dot(h, wcb_ref[...]) * _dot(h, wub_ref[...])
    tm = cu.shape[0]

    @pl.when(i % tiles_per_seq == 0)
    def _():
        carry_ref[...] = jnp.zeros_like(carry_ref)

    prev = carry_ref[...]
    carry_ref[...] = cu[tm - V7X_SUBLANES:, :]

    row8 = lax.broadcasted_iota(jnp.int32, prev.shape, 0)

    def shifted(s):
        rolled = pltpu.roll(cu, s, axis=0)
        head = jnp.where(row8 < s, pltpu.roll(prev, s, axis=0), rolled[:V7X_SUBLANES, :])
        return jnp.concatenate([head, rolled[V7X_SUBLANES:, :]], axis=0)

    cw = cw_ref[...]
    y = cw[2:3, :] * cu + cw[1:2, :] * shifted(1) + cw[0:1, :] * shifted(2)
    o_ref[...] = (_dot(h, wbb_ref[...]) * y).astype(o_ref.dtype)


def _conv_branch(h, w, conv_w, *, seq, tm, tn):
    t, d = h.shape
    nb = d // tn
    wspec = lambda k: pl.BlockSpec((d, tn), lambda j, i: (0, k * nb + j))
    return pl.pallas_call(
        functools.partial(_conv_kernel, tiles_per_seq=seq // tm),
        grid=(nb, t // tm),
        in_specs=[pl.BlockSpec((tm, d), lambda j, i: (i, 0)),
                  wspec(3), wspec(4), wspec(5),
                  pl.BlockSpec((CONV_K, tn), lambda j, i: (0, j))],
        out_specs=pl.BlockSpec((tm, tn), lambda j, i: (i, j)),
        out_shape=jax.ShapeDtypeStruct((t, d), BF16),
        scratch_shapes=[pltpu.VMEM((V7X_SUBLANES, tn), F32)] + [pltpu.VMEM((d, tn), BF16)] * 3,
        compiler_params=_params(2),
        name="conv_branch",
    )(h, w, w, w, conv_w)


def _diff_attn_kernel(*refs, tq, nq, lam_init, n_round):
    q_ref, k_ref, v_ref, lq1_ref, lk1_ref, lq2_ref, lk2_ref, sg_ref = refs[:8]
    o_ref = refs[8 + n_round]
    for src_ref, dst_ref in zip(refs[8:8 + n_round], refs[9 + n_round:]):
        dst_ref[...] = src_ref[...].astype(dst_ref.dtype)

    hd = DIFF_HEAD_DIM
    lam = (jnp.exp(jnp.sum(lq1_ref[...] * lk1_ref[...], axis=-1, keepdims=True))
           - jnp.exp(jnp.sum(lq2_ref[...] * lk2_ref[...], axis=-1, keepdims=True))
           + lam_init)
    sg = sg_ref[...]
    keep = (lax.broadcasted_iota(jnp.int32, (tq, tq), 1)
            <= lax.broadcasted_iota(jnp.int32, (tq, tq), 0))

    def softmax_terms(qc, kc, r0):
        s = _dot_nt(qc, kc)
        diag = jnp.where(keep, s[:, r0:], -jnp.inf)
        s = jnp.concatenate([s[:, :r0], diag], axis=1) if r0 else diag
        p = jnp.exp2(s - jnp.max(s, axis=-1, keepdims=True))
        return p, jnp.sum(p, axis=-1, keepdims=True)

    for qi in range(nq):
        r0 = qi * tq
        q = q_ref[r0:r0 + tq, :]
        k = k_ref[0:r0 + tq, :]
        p1, l1 = softmax_terms(q[:, :hd], k[:, :hd], r0)
        p2, l2 = softmax_terms(q[:, hd:], k[:, hd:], r0)
        v = v_ref[0:r0 + tq, :]
        o = _dot(p1.astype(BF16), v) / l1 - _dot(p2.astype(BF16), v) * (lam / l2)
        o = _rms(o, sg, SUBLN_EPS) * (1.0 - lam_init)
        o_ref[r0:r0 + tq, :] = o.astype(o_ref.dtype)


def _diff_attn(zqkv, lq1, lk1, lq2, lk2, subln, round_jobs, *, batch, seq, tq, lam_init):
    t = zqkv.shape[0]
    w = DIFF_PAIR
    lvec = pl.BlockSpec((1, DIFF_HEAD_DIM), lambda b, h: (0, 0))
    round_in, round_out, round_shapes = [], [], []
    for a, rb, cw, cblk in round_jobs:
        n_slab = pl.cdiv(a.shape[0], rb)
        assert n_slab <= batch * DIFF_HEADS and rb % (2 * V7X_SUBLANES) == 0
        slab = lambda b, h, n_slab=n_slab: jnp.minimum(b * DIFF_HEADS + h, n_slab - 1)
        round_in.append(pl.BlockSpec((rb, cw), lambda b, h, slab=slab, cblk=cblk: (slab(b, h), cblk)))
        round_out.append(pl.BlockSpec((rb, cw), lambda b, h, slab=slab: (slab(b, h), 0)))
        round_shapes.append(jax.ShapeDtypeStruct((a.shape[0], cw), BF16))
    res = pl.pallas_call(
        functools.partial(_diff_attn_kernel, tq=tq, nq=seq // tq, lam_init=lam_init,
                          n_round=len(round_jobs)),
        grid=(batch, DIFF_HEADS),
        in_specs=[pl.BlockSpec((seq, w), lambda b, h: (b, h)),
                  pl.BlockSpec((seq, w), lambda b, h: (b, DIFF_HEADS + h)),
                  pl.BlockSpec((seq, w), lambda b, h: (b, 2 * DIFF_HEADS + h)),
                  lvec, lvec, lvec, lvec,
                  pl.BlockSpec((1, w), lambda b, h: (0, 0))] + round_in,
        out_specs=[pl.BlockSpec((seq, w), lambda b, h: (b, h))] + round_out,
        out_shape=[jax.ShapeDtypeStruct((t, DIFF_HEADS * w), BF16)] + round_shapes,
        compiler_params=_params(2),
        name="diff_attn",
    )(zqkv, zqkv, zqkv, lq1, lk1, lq2, lk2, subln, *[job[0] for job in round_jobs])
    return res[0], res[1:]


def _merge_kernel(ya_ref, cv_ref, h_ref, wa_ref, wc_ref, wga_ref, wgc_ref, b_ref, o_ref):
    h = h_ref[...]
    b = b_ref[...]
    ga = _sigmoid(_dot(h, wga_ref[...].astype(BF16)) + b[0:1, :])
    gc = _sigmoid(_dot(h, wgc_ref[...].astype(BF16)) + b[1:2, :])
    m = (ga * _dot(ya_ref[...], wa_ref[...].astype(BF16))
         + gc * _dot(cv_ref[...], wc_ref[...].astype(BF16)))
    o_ref[...] = m.astype(o_ref.dtype)


def _merge(ya, cv, h, wa, wc, wga, wgc, b_gates, *, tm, tn):
    t, d = h.shape
    row = pl.BlockSpec((tm, d), lambda i, j: (i, 0))
    col = pl.BlockSpec((d, tn), lambda i, j: (0, j))
    return pl.pallas_call(
        _merge_kernel,
        grid=(t // tm, d // tn),
        in_specs=[row, row, row, col, col, col, col,
                  pl.BlockSpec((2, tn), lambda i, j: (0, j))],
        out_specs=pl.BlockSpec((tm, tn), lambda i, j: (i, j)),
        out_shape=jax.ShapeDtypeStruct((t, d), BF16),
        compiler_params=_params(2),
        name="gated_merge",
    )(ya, cv, h, wa, wc, wga, wgc, b_gates)


def _mix_out_kernel(m_ref, w_ref, x_ref, y_ref):
    y_ref[...] = x_ref[...] + _dot(m_ref[...], w_ref[...].astype(BF16))


def _mix_out(m, w, x, *, tm):
    t, d = x.shape
    row = pl.BlockSpec((tm, d), lambda i: (i, 0))
    return pl.pallas_call(
        _mix_out_kernel,
        grid=(t // tm,),
        in_specs=[row, pl.BlockSpec((d, d), lambda i: (0, 0)), row],
        out_specs=row,
        out_shape=jax.ShapeDtypeStruct((t, d), F32),
        compiler_params=_params(1),
        name="mix_out",
    )(m, w, x)


def _mem_kv_kernel(m_ref, g_ref, w_ref, o_ref, wb_ref):
    _cast_weights_once(pl.program_id(0), [(w_ref, wb_ref)])
    hm = _rms(m_ref[...], g_ref[...], NORM_EPS).astype(BF16)
    o_ref[...] = _dot(hm, wb_ref[...]).astype(o_ref.dtype)


def _mem_kv(mem, g, w, *, tm):
    t, d = mem.shape
    n = w.shape[1]
    return pl.pallas_call(
        _mem_kv_kernel,
        grid=(t // tm,),
        in_specs=[pl.BlockSpec((tm, d), lambda i: (i, 0)),
                  pl.BlockSpec((1, d), lambda i: (0, 0)),
                  pl.BlockSpec((d, n), lambda i: (0, 0))],
        out_specs=pl.BlockSpec((tm, n), lambda i: (i, 0)),
        out_shape=jax.ShapeDtypeStruct((t, n), BF16),
        scratch_shapes=[pltpu.VMEM((d, n), BF16)],
        compiler_params=_params(1),
        name="mem_kv",
    )(mem, g, w)


def _xattn_kernel(x_ref, g_ref, wq_ref, kv_ref, wo_ref, y_ref, wqb_ref, wob_ref):
    _cast_weights_once(pl.program_id(0), [(wq_ref, wqb_ref), (wo_ref, wob_ref)])
    hd = XATTN_HEAD_DIM
    x = x_ref[...]
    n = _rms(x, g_ref[...], NORM_EPS).astype(BF16)
    q = (_dot(n, wqb_ref[...]) * (hd ** -0.5)).astype(BF16)
    kv = kv_ref[...]
    outs = []
    for h in range(XATTN_HEADS):
        k = kv[:, h * hd:(h + 1) * hd]
        v = kv[:, XATTN_WIDTH + h * hd:XATTN_WIDTH + (h + 1) * hd]
        s = _dot_nt(q[:, h * hd:(h + 1) * hd], k)
        p = jnp.exp(s - jnp.max(s, axis=-1, keepdims=True))
        l = jnp.sum(p, axis=-1, keepdims=True)
        outs.append((_dot(p.astype(BF16), v) / l).astype(BF16))
    o = jnp.concatenate(outs, axis=-1)
    y_ref[...] = x + _dot(o, wob_ref[...])


def _xattn(x, g, wq, kv, wo, *, seq, tm):
    t, d = x.shape
    tiles_per_seq = seq // tm
    row = pl.BlockSpec((tm, d), lambda i: (i, 0))
    return pl.pallas_call(
        _xattn_kernel,
        grid=(t // tm,),
        in_specs=[row,
                  pl.BlockSpec((1, d), lambda i: (0, 0)),
                  pl.BlockSpec((d, XATTN_WIDTH), lambda i: (0, 0)),
                  pl.BlockSpec((MEM_LEN, 2 * XATTN_WIDTH), lambda i: (i // tiles_per_seq, 0)),
                  pl.BlockSpec((XATTN_WIDTH, d), lambda i: (0, 0))],
        out_specs=row,
        out_shape=jax.ShapeDtypeStruct((t, d), F32),
        scratch_shapes=[pltpu.VMEM((d, XATTN_WIDTH), BF16), pltpu.VMEM((XATTN_WIDTH, d), BF16)],
        compiler_params=_params(1),
        name="xattn",
    )(x, g, wq, kv, wo)


def kernel(x, mem, ffn1_norm, ffn1_w_gate, ffn1_w_up, ffn1_w_down, mix_norm, w_mix_in, b_gates,
           lambda_q1, lambda_k1, lambda_q2, lambda_k2, diff_subln, w_attn_out, conv_w, w_conv_out,
           w_mix_out, xattn_norm, mem_norm, w_xq, w_xkv, w_xo, ffn2_norm, ffn2_w_gate, ffn2_w_up,
           ffn2_w_down, final_norm):
    b, s, d = x.shape
    depth = ffn1_norm.shape[0]
    t = b * s

    xt = x.reshape(t, d)
    memt = mem.reshape(b * mem.shape[1], d)
    out = None
    for l in range(depth):
        last = l == depth - 1
        row = lambda a: a[l].reshape(1, -1)
        lam_init = 0.8 - 0.6 * math.exp(-0.3 * l)

        x1, h = _ffn(xt, row(ffn1_norm), ffn1_w_gate[l], ffn1_w_up[l], ffn1_w_down[l],
                     row(mix_norm), emit_y=True, tc=2048, tr=256, tf=V7X_MXU_DIM, tm=2048)

        w_in = w_mix_in[l]
        zqkv = _qkv(h, w_in, tm=2048, tn=1024)
        cv = _conv_branch(h, w_in, conv_w[l], seq=s, tm=1024, tn=512)
        slab = d // (b * DIFF_HEADS)
        ga_blk, gc_blk = 6, 7
        ya, (wa, wc, wga, wgc, wmo, w2g, w2u, w2d) = _diff_attn(
            zqkv, row(lambda_q1), row(lambda_k1), row(lambda_q2), row(lambda_k2), row(diff_subln),
            [(w_attn_out[l], slab, d, 0), (w_conv_out[l], slab, d, 0),
             (w_in, slab, d, ga_blk), (w_in, slab, d, gc_blk), (w_mix_out[l], slab, d, 0),
             (ffn2_w_gate[l], slab, ffn2_w_gate.shape[-1], 0),
             (ffn2_w_up[l], slab, ffn2_w_up.shape[-1], 0),
             (ffn2_w_down[l], V7X_MXU_DIM, d, 0)],
            batch=b, seq=s, tq=512, lam_init=lam_init)
        m = _merge(ya, cv, h, wa, wc, wga, wgc, b_gates[l], tm=1024, tn=512)
        x2 = _mix_out(m, wmo, x1, tm=512)

        kv = _mem_kv(memt, row(mem_norm), w_xkv[l], tm=512)
        x3 = _xattn(x2, row(xattn_norm), w_xq[l], kv, w_xo[l], seq=s, tm=512)

        g_next = final_norm.reshape(1, -1) if last else ffn1_norm[l + 1].reshape(1, -1)
        res = _ffn(x3, row(ffn2_norm), w2g, w2u, w2d, g_next, emit_y=not last,
                   tc=2048, tr=256, tf=2 * V7X_MXU_DIM, tm=1024)
        if last:
            out = res
        else:
            xt = res[0]
    return out.reshape(b, s, d)
```

```python
import functools
import math

import jax
import jax.numpy as jnp
from jax import lax
from jax.experimental import pallas as pl
from jax.experimental.pallas import tpu as pltpu

D_MODEL = 2048
MEM_LEN = 256
DIFF_HEADS = 8
DIFF_HEAD_DIM = 128
DIFF_PAIR = 2 * DIFF_HEAD_DIM
CONV_K = 3
XATTN_HEADS = 4
XATTN_HEAD_DIM = 128
XATTN_WIDTH = XATTN_HEADS * XATTN_HEAD_DIM
NORM_EPS = 1e-6
SUBLN_EPS = 1e-5
LOG2_E = math.log2(math.e)

V7X_LANES = 128
V7X_SUBLANES = 8
V7X_MXU_DIM = 256
V7X_VMEM_LIMIT_BYTES = 56 * 1024 * 1024

BF16 = jnp.bfloat16
F32 = jnp.float32


def _params(n_axes):
    return pltpu.CompilerParams(
        dimension_semantics=("arbitrary",) * n_axes,
        vmem_limit_bytes=V7X_VMEM_LIMIT_BYTES)


def _dot(a, b):
    return jnp.dot(a, b, preferred_element_type=F32)


def _dot_nt(a, b):
    return lax.dot_general(a, b, (((1,), (1,)), ((), ())), preferred_element_type=F32)


def _rms(x, g, eps):
    return x * lax.rsqrt(jnp.mean(x * x, axis=-1, keepdims=True) + eps) * g


def _sigmoid(x):
    return 1.0 / (1.0 + jnp.exp(-x))


def _cast_weights_once(step, pairs):
    @pl.when(step == 0)
    def _():
        for src, dst in pairs:
            dst[...] = src[...].astype(dst.dtype)


def _norm_kernel(x_ref, g_ref, o_ref):
    o_ref[...] = _rms(x_ref[...], g_ref[...], NORM_EPS).astype(o_ref.dtype)


def _norm(x, g, *, tm):
    t, d = x.shape
    row = pl.BlockSpec((tm, d), lambda i: (i, 0))
    return pl.pallas_call(
        _norm_kernel,
        grid=(t // tm,),
        in_specs=[row, pl.BlockSpec((1, d), lambda i: (0, 0))],
        out_specs=row,
        out_shape=jax.ShapeDtypeStruct((t, d), BF16),
        compiler_params=_params(1),
        name="ffn_norm",
    )(x, g)


def _ffn_up_kernel(h_ref, wg_ref, wu_ref, wd_ref, act_ref, wdb_ref, wgb_ref, wub_ref, *, ts):
    _cast_weights_once(pl.program_id(1), [(wg_ref, wgb_ref), (wu_ref, wub_ref)])
    wdb_ref[...] = wd_ref[...].astype(wdb_ref.dtype)
    wg, wu = wgb_ref[...], wub_ref[...]
    for r0 in range(0, h_ref.shape[0], ts):
        h = h_ref[r0:r0 + ts, :]
        gate = _dot(h, wg)
        act_ref[r0:r0 + ts, :] = (gate * _sigmoid(gate) * _dot(h, wu)).astype(act_ref.dtype)


def _ffn_up(h, wg, wu, wd, *, tm, tn, ts, slab):
    t, d = h.shape
    f = wg.shape[1]
    n_j, n_i = pl.cdiv(f, tn), t // tm
    n_slab = pl.cdiv(f, slab)
    assert n_slab <= n_j * n_i
    slab_idx = lambda j, i: (jnp.minimum(j * n_i + i, n_slab - 1), 0)
    col = pl.BlockSpec((d, tn), lambda j, i: (0, j))
    return pl.pallas_call(
        functools.partial(_ffn_up_kernel, ts=ts),
        grid=(n_j, n_i),
        in_specs=[pl.BlockSpec((tm, d), lambda j, i: (i, 0)), col, col,
                  pl.BlockSpec((slab, d), slab_idx)],
        out_specs=(pl.BlockSpec((tm, tn), lambda j, i: (i, j)),
                   pl.BlockSpec((slab, d), slab_idx)),
        out_shape=(jax.ShapeDtypeStruct((t, f), BF16), jax.ShapeDtypeStruct((f, d), BF16)),
        scratch_shapes=[pltpu.VMEM((d, tn), BF16)] * 2,
        compiler_params=_params(2),
        name="ffn_up",
    )(h, wg, wu, wd)


def _ffn_down_kernel(act_ref, wd_ref, x_ref, g2_ref, *out_refs, emit_y):
    y = x_ref[...] + 0.5 * _dot(act_ref[...], wd_ref[...])
    if emit_y:
        out_refs[0][...] = y
    out_refs[-1][...] = _rms(y, g2_ref[...], NORM_EPS).astype(out_refs[-1].dtype)


def _ffn_down(act, wd, x, g2, *, emit_y, tm):
    t, d = x.shape
    f = act.shape[1]
    row = pl.BlockSpec((tm, d), lambda i: (i, 0))
    in_specs = [pl.BlockSpec((tm, f), lambda i: (i, 0)),
                pl.BlockSpec((f, d), lambda i: (0, 0), pipeline_mode=pl.Buffered(1)),
                row, pl.BlockSpec((1, d), lambda i: (0, 0))]
    if emit_y:
        out_shape = (jax.ShapeDtypeStruct((t, d), F32), jax.ShapeDtypeStruct((t, d), BF16))
        out_specs = (row, row)
    else:
        out_shape = jax.ShapeDtypeStruct((t, d), F32)
        out_specs = row
    return pl.pallas_call(
        functools.partial(_ffn_down_kernel, emit_y=emit_y),
        grid=(t // tm,),
        in_specs=in_specs, out_specs=out_specs, out_shape=out_shape,
        compiler_params=_params(1),
        name="ffn_down_y" if emit_y else "ffn_down_final",
    )(act, wd, x, g2)


def _qkv_kernel(h_ref, w_ref, o_ref, wb_ref, *, q_blocks, scale):
    j = pl.program_id(0)
    _cast_weights_once(pl.program_id(1), [(w_ref, wb_ref)])
    acc = _dot(h_ref[...], wb_ref[...])
    acc = acc * jnp.where(j < q_blocks, scale, 1.0).astype(F32)
    o_ref[...] = acc.astype(o_ref.dtype)


def _qkv(h, w, *, tm, tn):
    t, d = h.shape
    n = 3 * d
    return pl.pallas_call(
        functools.partial(_qkv_kernel, q_blocks=d // tn, scale=DIFF_HEAD_DIM ** -0.5 * LOG2_E),
        grid=(n // tn, t // tm),
        in_specs=[pl.BlockSpec((tm, d), lambda j, i: (i, 0)),
                  pl.BlockSpec((d, tn), lambda j, i: (0, j))],
        out_specs=pl.BlockSpec((tm, tn), lambda j, i: (i, j)),
        out_shape=jax.ShapeDtypeStruct((t, n), BF16),
        scratch_shapes=[pltpu.VMEM((d, tn), BF16)],
        compiler_params=_params(2),
        name="qkv_proj",
    )(h, w)


def _conv_kernel(h_ref, wb_ref, wc_ref, wu_ref, cw_ref, o_ref, carry_ref, wbb_ref, wcb_ref, wub_ref,
                 *, tiles_per_seq, ts):
    i = pl.program_id(1)
    _cast_weights_once(i, [(wb_ref, wbb_ref), (wc_ref, wcb_ref), (wu_ref, wub_ref)])

    @pl.when(i % tiles_per_seq == 0)
    def _():
        carry_ref[...] = jnp.zeros_like(carry_ref)

    prev = carry_ref[...]
    row8 = lax.broadcasted_iota(jnp.int32, prev.shape, 0)
    cw = cw_ref[...]
    wb, wc, wu = wbb_ref[...], wcb_ref[...], wub_ref[...]

    for r0 in range(0, h_ref.shape[0], ts):
        h = h_ref[r0:r0 + ts, :]
        cu = _dot(h, wc) * _dot(h, wu)

        def shifted(s):
            rolled = pltpu.roll(cu, s, axis=0)
            head = jnp.where(row8 < s, pltpu.roll(prev, s, axis=0), rolled[:V7X_SUBLANES, :])
            return jnp.concatenate([head, rolled[V7X_SUBLANES:, :]], axis=0)

        y = cw[2:3, :] * cu + cw[1:2, :] * shifted(1) + cw[0:1, :] * shifted(2)
        o_ref[r0:r0 + ts, :] = (_dot(h, wb) * y).astype(o_ref.dtype)
        prev = cu[ts - V7X_SUBLANES:, :]
    carry_ref[...] = prev


def _conv_branch(h, w, conv_w, *, seq, tm, tn, ts):
    t, d = h.shape
    nb = d // tn
    wspec = lambda k: pl.BlockSpec((d, tn), lambda j, i: (0, k * nb + j))
    return pl.pallas_call(
        functools.partial(_conv_kernel, tiles_per_seq=seq // tm, ts=ts),
        grid=(nb, t // tm),
        in_specs=[pl.BlockSpec((tm, d), lambda j, i: (i, 0)),
                  wspec(3), wspec(4), wspec(5),
                  pl.BlockSpec((CONV_K, tn), lambda j, i: (0, j))],
        out_specs=pl.BlockSpec((tm, tn), lambda j, i: (i, j)),
        out_shape=jax.ShapeDtypeStruct((t, d), BF16),
        scratch_shapes=[pltpu.VMEM((V7X_SUBLANES, tn), F32)] + [pltpu.VMEM((d, tn), BF16)] * 3,
        compiler_params=_params(2),
        name="conv_branch",
    )(h, w, w, w, conv_w)


def _diff_attn_kernel(*refs, tq, nq, lam_init, n_round):
    q_ref, k_ref, v_ref, lq1_ref, lk1_ref, lq2_ref, lk2_ref, sg_ref = refs[:8]
    o_ref = refs[8 + n_round]
    for src_ref, dst_ref in zip(refs[8:8 + n_round], refs[9 + n_round:]):
        dst_ref[...] = src_ref[...].astype(dst_ref.dtype)

    hd = DIFF_HEAD_DIM
    lam = (jnp.exp(jnp.sum(lq1_ref[...] * lk1_ref[...], axis=-1, keepdims=True))
           - jnp.exp(jnp.sum(lq2_ref[...] * lk2_ref[...], axis=-1, keepdims=True))
           + lam_init)
    sg = sg_ref[...]
    keep = (lax.broadcasted_iota(jnp.int32, (tq, tq), 1)
            <= lax.broadcasted_iota(jnp.int32, (tq, tq), 0))

    def softmax_terms(qc, kc, r0):
        s = _dot_nt(qc, kc)
        diag = jnp.where(keep, s[:, r0:], -jnp.inf)
        s = jnp.concatenate([s[:, :r0], diag], axis=1) if r0 else diag
        p = jnp.exp2(s - jnp.max(s, axis=-1, keepdims=True))
        return p, jnp.sum(p, axis=-1, keepdims=True)

    for qi in range(nq):
        r0 = qi * tq
        q = q_ref[r0:r0 + tq, :]
        k = k_ref[0:r0 + tq, :]
        p1, l1 = softmax_terms(q[:, :hd], k[:, :hd], r0)
        p2, l2 = softmax_terms(q[:, hd:], k[:, hd:], r0)
        v = v_ref[0:r0 + tq, :]
        o = _dot(p1.astype(BF16), v) / l1 - _dot(p2.astype(BF16), v) * (lam / l2)
        o = _rms(o, sg, SUBLN_EPS) * (1.0 - lam_init)
        o_ref[r0:r0 + tq, :] = o.astype(o_ref.dtype)


def _diff_attn(zqkv, lq1, lk1, lq2, lk2, subln, round_jobs, *, batch, seq, tq, lam_init):
    t = zqkv.shape[0]
    w = DIFF_PAIR
    lvec = pl.BlockSpec((1, DIFF_HEAD_DIM), lambda b, h: (0, 0))
    round_in, round_out, round_shapes = [], [], []
    for a, rb, cw, cblk in round_jobs:
        n_slab = pl.cdiv(a.shape[0], rb)
        assert n_slab <= batch * DIFF_HEADS and rb % (2 * V7X_SUBLANES) == 0
        slab = lambda b, h, n_slab=n_slab: jnp.minimum(b * DIFF_HEADS + h, n_slab - 1)
        round_in.append(pl.BlockSpec((rb, cw), lambda b, h, slab=slab, cblk=cblk: (slab(b, h), cblk)))
        round_out.append(pl.BlockSpec((rb, cw), lambda b, h, slab=slab: (slab(b, h), 0)))
        round_shapes.append(jax.ShapeDtypeStruct((a.shape[0], cw), BF16))
    res = pl.pallas_call(
        functools.partial(_diff_attn_kernel, tq=tq, nq=seq // tq, lam_init=lam_init,
                          n_round=len(round_jobs)),
        grid=(batch, DIFF_HEADS),
        in_specs=[pl.BlockSpec((seq, w), lambda b, h: (b, h)),
                  pl.BlockSpec((seq, w), lambda b, h: (b, DIFF_HEADS + h)),
                  pl.BlockSpec((seq, w), lambda b, h: (b, 2 * DIFF_HEADS + h)),
                  lvec, lvec, lvec, lvec,
                  pl.BlockSpec((1, w), lambda b, h: (0, 0))] + round_in,
        out_specs=[pl.BlockSpec((seq, w), lambda b, h: (b, h))] + round_out,
        out_shape=[jax.ShapeDtypeStruct((t, DIFF_HEADS * w), BF16)] + round_shapes,
        compiler_params=_params(2),
        name="diff_attn",
    )(zqkv, zqkv, zqkv, lq1, lk1, lq2, lk2, subln, *[job[0] for job in round_jobs])
    return res[0], res[1:]


def _merge_kernel(ya_ref, cv_ref, h_ref, wa_ref, wc_ref, wga_ref, wgc_ref, b_ref, o_ref, *, ts):
    b = b_ref[...]
    wa, wc = wa_ref[...].astype(BF16), wc_ref[...].astype(BF16)
    wga, wgc = wga_ref[...].astype(BF16), wgc_ref[...].astype(BF16)
    for r0 in range(0, h_ref.shape[0], ts):
        h = h_ref[r0:r0 + ts, :]
        ga = _sigmoid(_dot(h, wga) + b[0:1, :])
        gc = _sigmoid(_dot(h, wgc) + b[1:2, :])
        m = ga * _dot(ya_ref[r0:r0 + ts, :], wa) + gc * _dot(cv_ref[r0:r0 + ts, :], wc)
        o_ref[r0:r0 + ts, :] = m.astype(o_ref.dtype)


def _merge(ya, cv, h, wa, wc, wga, wgc, b_gates, *, tm, tn, ts):
    t, d = h.shape
    row = pl.BlockSpec((tm, d), lambda i, j: (i, 0))
    col = pl.BlockSpec((d, tn), lambda i, j: (0, j))
    return pl.pallas_call(
        functools.partial(_merge_kernel, ts=ts),
        grid=(t // tm, d // tn),
        in_specs=[row, row, row, col, col, col, col,
                  pl.BlockSpec((2, tn), lambda i, j: (0, j))],
        out_specs=pl.BlockSpec((tm, tn), lambda i, j: (i, j)),
        out_shape=jax.ShapeDtypeStruct((t, d), BF16),
        compiler_params=_params(2),
        name="gated_merge",
    )(ya, cv, h, wa, wc, wga, wgc, b_gates)


def _mix_out_kernel(m_ref, w_ref, x_ref, y_ref):
    y_ref[...] = x_ref[...] + _dot(m_ref[...], w_ref[...].astype(BF16))


def _mix_out(m, w, x, *, tm):
    t, d = x.shape
    row = pl.BlockSpec((tm, d), lambda i: (i, 0))
    return pl.pallas_call(
        _mix_out_kernel,
        grid=(t // tm,),
        in_specs=[row, pl.BlockSpec((d, d), lambda i: (0, 0)), row],
        out_specs=row,
        out_shape=jax.ShapeDtypeStruct((t, d), F32),
        compiler_params=_params(1),
        name="mix_out",
    )(m, w, x)


def _mem_kv_kernel(m_ref, g_ref, w_ref, o_ref, wb_ref):
    _cast_weights_once(pl.program_id(0), [(w_ref, wb_ref)])
    hm = _rms(m_ref[...], g_ref[...], NORM_EPS).astype(BF16)
    o_ref[...] = _dot(hm, wb_ref[...]).astype(o_ref.dtype)


def _mem_kv(mem, g, w, *, tm):
    t, d = mem.shape
    n = w.shape[1]
    return pl.pallas_call(
        _mem_kv_kernel,
        grid=(t // tm,),
        in_specs=[pl.BlockSpec((tm, d), lambda i: (i, 0)),
                  pl.BlockSpec((1, d), lambda i: (0, 0)),
                  pl.BlockSpec((d, n), lambda i: (0, 0))],
        out_specs=pl.BlockSpec((tm, n), lambda i: (i, 0)),
        out_shape=jax.ShapeDtypeStruct((t, n), BF16),
        scratch_shapes=[pltpu.VMEM((d, n), BF16)],
        compiler_params=_params(1),
        name="mem_kv",
    )(mem, g, w)


def _xattn_kernel(x_ref, g_ref, wq_ref, kv_ref, wo_ref, g2_ref, y_ref, n_ref, wqb_ref, wob_ref):
    _cast_weights_once(pl.program_id(0), [(wq_ref, wqb_ref), (wo_ref, wob_ref)])
    hd = XATTN_HEAD_DIM
    x = x_ref[...]
    n = _rms(x, g_ref[...], NORM_EPS).astype(BF16)
    q = (_dot(n, wqb_ref[...]) * (hd ** -0.5)).astype(BF16)
    kv = kv_ref[...]
    outs = []
    for h in range(XATTN_HEADS):
        k = kv[:, h * hd:(h + 1) * hd]
        v = kv[:, XATTN_WIDTH + h * hd:XATTN_WIDTH + (h + 1) * hd]
        s = _dot_nt(q[:, h * hd:(h + 1) * hd], k)
        p = jnp.exp(s - jnp.max(s, axis=-1, keepdims=True))
        l = jnp.sum(p, axis=-1, keepdims=True)
        outs.append((_dot(p.astype(BF16), v) / l).astype(BF16))
    o = jnp.concatenate(outs, axis=-1)
    y = x + _dot(o, wob_ref[...])
    y_ref[...] = y
    n_ref[...] = _rms(y, g2_ref[...], NORM_EPS).astype(n_ref.dtype)


def _xattn(x, g, wq, kv, wo, g2, *, seq, tm):
    t, d = x.shape
    tiles_per_seq = seq // tm
    row = pl.BlockSpec((tm, d), lambda i: (i, 0))
    return pl.pallas_call(
        _xattn_kernel,
        grid=(t // tm,),
        in_specs=[row,
                  pl.BlockSpec((1, d), lambda i: (0, 0)),
                  pl.BlockSpec((d, XATTN_WIDTH), lambda i: (0, 0)),
                  pl.BlockSpec((MEM_LEN, 2 * XATTN_WIDTH), lambda i: (i // tiles_per_seq, 0)),
                  pl.BlockSpec((XATTN_WIDTH, d), lambda i: (0, 0)),
                  pl.BlockSpec((1, d), lambda i: (0, 0))],
        out_specs=(row, row),
        out_shape=(jax.ShapeDtypeStruct((t, d), F32), jax.ShapeDtypeStruct((t, d), BF16)),
        scratch_shapes=[pltpu.VMEM((d, XATTN_WIDTH), BF16), pltpu.VMEM((XATTN_WIDTH, d), BF16)],
        compiler_params=_params(1),
        name="xattn",
    )(x, g, wq, kv, wo, g2)


def kernel(x, mem, ffn1_norm, ffn1_w_gate, ffn1_w_up, ffn1_w_down, mix_norm, w_mix_in, b_gates,
           lambda_q1, lambda_k1, lambda_q2, lambda_k2, diff_subln, w_attn_out, conv_w, w_conv_out,
           w_mix_out, xattn_norm, mem_norm, w_xq, w_xkv, w_xo, ffn2_norm, ffn2_w_gate, ffn2_w_up,
           ffn2_w_down, final_norm):
    b, s, d = x.shape
    depth = ffn1_norm.shape[0]
    t = b * s

    xt = x.reshape(t, d)
    memt = mem.reshape(b * mem.shape[1], d)
    out = None
    for l in range(depth):
        last = l == depth - 1
        row = lambda a: a[l].reshape(1, -1)
        lam_init = 0.8 - 0.6 * math.exp(-0.3 * l)

        ffn_up = dict(tm=1024, tn=2 * V7X_MXU_DIM, ts=256, slab=V7X_LANES)
        act, wdb = _ffn_up(_norm(xt, row(ffn1_norm), tm=1024), ffn1_w_gate[l], ffn1_w_up[l],
                           ffn1_w_down[l], **ffn_up)
        x1, h = _ffn_down(act, wdb, xt, row(mix_norm), emit_y=True, tm=256)

        w_in = w_mix_in[l]
        zqkv = _qkv(h, w_in, tm=2048, tn=1024)
        cv = _conv_branch(h, w_in, conv_w[l], seq=s, tm=1024, tn=512, ts=256)
        slab = d // (b * DIFF_HEADS)
        ga_blk, gc_blk = 6, 7
        ya, (wa, wc, wga, wgc, wmo) = _diff_attn(
            zqkv, row(lambda_q1), row(lambda_k1), row(lambda_q2), row(lambda_k2), row(diff_subln),
            [(w_attn_out[l], slab, d, 0), (w_conv_out[l], slab, d, 0),
             (w_in, slab, d, ga_blk), (w_in, slab, d, gc_blk), (w_mix_out[l], slab, d, 0)],
            batch=b, seq=s, tq=512, lam_init=lam_init)
        m = _merge(ya, cv, h, wa, wc, wga, wgc, b_gates[l], tm=1024, tn=512, ts=256)
        x2 = _mix_out(m, wmo, x1, tm=512)

        kv = _mem_kv(memt, row(mem_norm), w_xkv[l], tm=512)
        x3, h3 = _xattn(x2, row(xattn_norm), w_xq[l], kv, w_xo[l], row(ffn2_norm), seq=s, tm=512)

        g_next = final_norm.reshape(1, -1) if last else ffn1_norm[l + 1].reshape(1, -1)
        act, wdb = _ffn_up(h3, ffn2_w_gate[l], ffn2_w_up[l], ffn2_w_down[l], **ffn_up)
        res = _ffn_down(act, wdb, x3, g_next, emit_y=not last, tm=256)
        if last:
            out = res
        else:
            xt = res[0]
    return out.reshape(b, s, d)
```

```python
import functools
import math

import jax
import jax.numpy as jnp
from jax import lax
from jax.experimental import pallas as pl
from jax.experimental.pallas import tpu as pltpu

D_MODEL = 2048
MEM_LEN = 256
DIFF_HEADS = 8
DIFF_HEAD_DIM = 128
DIFF_PAIR = 2 * DIFF_HEAD_DIM
CONV_K = 3
XATTN_HEADS = 4
XATTN_HEAD_DIM = 128
XATTN_WIDTH = XATTN_HEADS * XATTN_HEAD_DIM
NORM_EPS = 1e-6
SUBLN_EPS = 1e-5
LOG2_E = math.log2(math.e)

V7X_LANES = 128
V7X_SUBLANES = 8
V7X_MXU_DIM = 256
V7X_VMEM_LIMIT_BYTES = 56 * 1024 * 1024

BF16 = jnp.bfloat16
F32 = jnp.float32


def _params(n_axes):
    return pltpu.CompilerParams(
        dimension_semantics=("arbitrary",) * n_axes,
        vmem_limit_bytes=V7X_VMEM_LIMIT_BYTES)


def _dot(a, b):
    return jnp.dot(a, b, preferred_element_type=F32)


def _dot_nt(a, b):
    return lax.dot_general(a, b, (((1,), (1,)), ((), ())), preferred_element_type=F32)


def _rms(x, g, eps):
    return x * lax.rsqrt(jnp.mean(x * x, axis=-1, keepdims=True) + eps) * g


def _sigmoid(x):
    return 1.0 / (1.0 + jnp.exp(-x))


def _cast_weights_once(step, pairs):
    @pl.when(step == 0)
    def _():
        for src, dst in pairs:
            dst[...] = src[...].astype(dst.dtype)


def _norm_kernel(x_ref, g_ref, o_ref):
    o_ref[...] = _rms(x_ref[...], g_ref[...], NORM_EPS).astype(o_ref.dtype)


def _norm(x, g, *, tm):
    t, d = x.shape
    row = pl.BlockSpec((tm, d), lambda i: (i, 0))
    return pl.pallas_call(
        _norm_kernel,
        grid=(t // tm,),
        in_specs=[row, pl.BlockSpec((1, d), lambda i: (0, 0))],
        out_specs=row,
        out_shape=jax.ShapeDtypeStruct((t, d), BF16),
        compiler_params=_params(1),
        name="ffn_norm",
    )(x, g)


def _ffn_up_kernel(h_ref, wg_ref, wu_ref, wd_ref, act_ref, wdb_ref, wgb_ref, wub_ref, *, ts):
    _cast_weights_once(pl.program_id(1), [(wg_ref, wgb_ref), (wu_ref, wub_ref)])
    wdb_ref[...] = wd_ref[...].astype(wdb_ref.dtype)
    wg, wu = wgb_ref[...], wub_ref[...]
    for r0 in range(0, h_ref.shape[0], ts):
        h = h_ref[r0:r0 + ts, :]
        gate = _dot(h, wg)
        act_ref[r0:r0 + ts, :] = (gate * _sigmoid(gate) * _dot(h, wu)).astype(act_ref.dtype)


def _ffn_up(h, wg, wu, wd, *, tm, tn, ts, slab):
    t, d = h.shape
    f = wg.shape[1]
    n_j, n_i = pl.cdiv(f, tn), t // tm
    n_slab = pl.cdiv(f, slab)
    assert n_slab <= n_j * n_i
    slab_idx = lambda j, i: (jnp.minimum(j * n_i + i, n_slab - 1), 0)
    col = pl.BlockSpec((d, tn), lambda j, i: (0, j))
    return pl.pallas_call(
        functools.partial(_ffn_up_kernel, ts=ts),
        grid=(n_j, n_i),
        in_specs=[pl.BlockSpec((tm, d), lambda j, i: (i, 0)), col, col,
                  pl.BlockSpec((slab, d), slab_idx)],
        out_specs=(pl.BlockSpec((tm, tn), lambda j, i: (i, j)),
                   pl.BlockSpec((slab, d), slab_idx)),
        out_shape=(jax.ShapeDtypeStruct((t, f), BF16), jax.ShapeDtypeStruct((f, d), BF16)),
        scratch_shapes=[pltpu.VMEM((d, tn), BF16)] * 2,
        compiler_params=_params(2),
        name="ffn_up",
    )(h, wg, wu, wd)


def _ffn_down_kernel(act_ref, wd_ref, x_ref, g2_ref, *out_refs, emit_y, ts):
    g2 = g2_ref[...]
    for r0 in range(0, x_ref.shape[0], ts):
        y = x_ref[r0:r0 + ts, :] + 0.5 * _dot(act_ref[r0:r0 + ts, :], wd_ref[...])
        if emit_y:
            out_refs[0][r0:r0 + ts, :] = y
        out_refs[-1][r0:r0 + ts, :] = _rms(y, g2, NORM_EPS).astype(out_refs[-1].dtype)


def _ffn_down(act, wd, x, g2, *, emit_y, tm, ts):
    t, d = x.shape
    f = act.shape[1]
    row = pl.BlockSpec((tm, d), lambda i: (i, 0))
    in_specs = [pl.BlockSpec((tm, f), lambda i: (i, 0)),
                pl.BlockSpec((f, d), lambda i: (0, 0), pipeline_mode=pl.Buffered(1)),
                row, pl.BlockSpec((1, d), lambda i: (0, 0))]
    if emit_y:
        out_shape = (jax.ShapeDtypeStruct((t, d), F32), jax.ShapeDtypeStruct((t, d), BF16))
        out_specs = (row, row)
    else:
        out_shape = jax.ShapeDtypeStruct((t, d), F32)
        out_specs = row
    return pl.pallas_call(
        functools.partial(_ffn_down_kernel, emit_y=emit_y, ts=ts),
        grid=(t // tm,),
        in_specs=in_specs, out_specs=out_specs, out_shape=out_shape,
        compiler_params=_params(1),
        name="ffn_down_y" if emit_y else "ffn_down_final",
    )(act, wd, x, g2)


def _qkv_kernel(h_ref, w_ref, o_ref, wb_ref, *, q_blocks, scale):
    j = pl.program_id(0)
    _cast_weights_once(pl.program_id(1), [(w_ref, wb_ref)])
    acc = _dot(h_ref[...], wb_ref[...])
    acc = acc * jnp.where(j < q_blocks, scale, 1.0).astype(F32)
    o_ref[...] = acc.astype(o_ref.dtype)


def _qkv(h, w, *, tm, tn):
    t, d = h.shape
    n = 3 * d
    return pl.pallas_call(
        functools.partial(_qkv_kernel, q_blocks=d // tn, scale=DIFF_HEAD_DIM ** -0.5 * LOG2_E),
        grid=(n // tn, t // tm),
        in_specs=[pl.BlockSpec((tm, d), lambda j, i: (i, 0)),
                  pl.BlockSpec((d, tn), lambda j, i: (0, j))],
        out_specs=pl.BlockSpec((tm, tn), lambda j, i: (i, j)),
        out_shape=jax.ShapeDtypeStruct((t, n), BF16),
        scratch_shapes=[pltpu.VMEM((d, tn), BF16)],
        compiler_params=_params(2),
        name="qkv_proj",
    )(h, w)


def _conv_kernel(h_ref, wb_ref, wc_ref, wu_ref, cw_ref, o_ref, carry_ref, wbb_ref, wcb_ref, wub_ref,
                 *, tiles_per_seq, ts):
    i = pl.program_id(1)
    _cast_weights_once(i, [(wb_ref, wbb_ref), (wc_ref, wcb_ref), (wu_ref, wub_ref)])

    @pl.when(i % tiles_per_seq == 0)
    def _():
        carry_ref[...] = jnp.zeros_like(carry_ref)

    prev = carry_ref[...]
    row8 = lax.broadcasted_iota(jnp.int32, prev.shape, 0)
    cw = cw_ref[...]
    wb, wc, wu = wbb_ref[...], wcb_ref[...], wub_ref[...]

    for r0 in range(0, h_ref.shape[0], ts):
        h = h_ref[r0:r0 + ts, :]
        cu = _dot(h, wc) * _dot(h, wu)

        def shifted(s):
            rolled = pltpu.roll(cu, s, axis=0)
            head = jnp.where(row8 < s, pltpu.roll(prev, s, axis=0), rolled[:V7X_SUBLANES, :])
            return jnp.concatenate([head, rolled[V7X_SUBLANES:, :]], axis=0)

        y = cw[2:3, :] * cu + cw[1:2, :] * shifted(1) + cw[0:1, :] * shifted(2)
        o_ref[r0:r0 + ts, :] = (_dot(h, wb) * y).astype(o_ref.dtype)
        prev = cu[ts - V7X_SUBLANES:, :]
    carry_ref[...] = prev


def _conv_branch(h, w, conv_w, *, seq, tm, tn, ts):
    t, d = h.shape
    nb = d // tn
    wspec = lambda k: pl.BlockSpec((d, tn), lambda j, i: (0, k * nb + j))
    return pl.pallas_call(
        functools.partial(_conv_kernel, tiles_per_seq=seq // tm, ts=ts),
        grid=(nb, t // tm),
        in_specs=[pl.BlockSpec((tm, d), lambda j, i: (i, 0)),
                  wspec(3), wspec(4), wspec(5),
                  pl.BlockSpec((CONV_K, tn), lambda j, i: (0, j))],
        out_specs=pl.BlockSpec((tm, tn), lambda j, i: (i, j)),
        out_shape=jax.ShapeDtypeStruct((t, d), BF16),
        scratch_shapes=[pltpu.VMEM((V7X_SUBLANES, tn), F32)] + [pltpu.VMEM((d, tn), BF16)] * 3,
        compiler_params=_params(2),
        name="conv_branch",
    )(h, w, w, w, conv_w)


def _diff_attn_kernel(*refs, tq, nq, lam_init, n_round):
    q_ref, k_ref, v_ref, lq1_ref, lk1_ref, lq2_ref, lk2_ref, sg_ref = refs[:8]
    o_ref = refs[8 + n_round]
    for src_ref, dst_ref in zip(refs[8:8 + n_round], refs[9 + n_round:]):
        dst_ref[...] = src_ref[...].astype(dst_ref.dtype)

    hd = DIFF_HEAD_DIM
    lam = (jnp.exp(jnp.sum(lq1_ref[...] * lk1_ref[...], axis=-1, keepdims=True))
           - jnp.exp(jnp.sum(lq2_ref[...] * lk2_ref[...], axis=-1, keepdims=True))
           + lam_init)
    sg = sg_ref[...]
    keep = (lax.broadcasted_iota(jnp.int32, (tq, tq), 1)
            <= lax.broadcasted_iota(jnp.int32, (tq, tq), 0))

    def softmax_terms(qc, kc, r0):
        s = _dot_nt(qc, kc)
        diag = jnp.where(keep, s[:, r0:], -jnp.inf)
        s = jnp.concatenate([s[:, :r0], diag], axis=1) if r0 else diag
        p = jnp.exp2(s - jnp.max(s, axis=-1, keepdims=True))
        return p, jnp.sum(p, axis=-1, keepdims=True)

    for qi in range(nq):
        r0 = qi * tq
        q = q_ref[r0:r0 + tq, :]
        k = k_ref[0:r0 + tq, :]
        p1, l1 = softmax_terms(q[:, :hd], k[:, :hd], r0)
        p2, l2 = softmax_terms(q[:, hd:], k[:, hd:], r0)
        v = v_ref[0:r0 + tq, :]
        o = _dot(p1.astype(BF16), v) / l1 - _dot(p2.astype(BF16), v) * (lam / l2)
        o = _rms(o, sg, SUBLN_EPS) * (1.0 - lam_init)
        o_ref[r0:r0 + tq, :] = o.astype(o_ref.dtype)


def _diff_attn(zqkv, lq1, lk1, lq2, lk2, subln, round_jobs, *, batch, seq, tq, lam_init):
    t = zqkv.shape[0]
    w = DIFF_PAIR
    lvec = pl.BlockSpec((1, DIFF_HEAD_DIM), lambda b, h: (0, 0))
    round_in, round_out, round_shapes = [], [], []
    for a, rb, cw, cblk in round_jobs:
        n_slab = pl.cdiv(a.shape[0], rb)
        assert n_slab <= batch * DIFF_HEADS and rb % (2 * V7X_SUBLANES) == 0
        slab = lambda b, h, n_slab=n_slab: jnp.minimum(b * DIFF_HEADS + h, n_slab - 1)
        round_in.append(pl.BlockSpec((rb, cw), lambda b, h, slab=slab, cblk=cblk: (slab(b, h), cblk)))
        round_out.append(pl.BlockSpec((rb, cw), lambda b, h, slab=slab: (slab(b, h), 0)))
        round_shapes.append(jax.ShapeDtypeStruct((a.shape[0], cw), BF16))
    res = pl.pallas_call(
        functools.partial(_diff_attn_kernel, tq=tq, nq=seq // tq, lam_init=lam_init,
                          n_round=len(round_jobs)),
        grid=(batch, DIFF_HEADS),
        in_specs=[pl.BlockSpec((seq, w), lambda b, h: (b, h)),
                  pl.BlockSpec((seq, w), lambda b, h: (b, DIFF_HEADS + h)),
                  pl.BlockSpec((seq, w), lambda b, h: (b, 2 * DIFF_HEADS + h)),
                  lvec, lvec, lvec, lvec,
                  pl.BlockSpec((1, w), lambda b, h: (0, 0))] + round_in,
        out_specs=[pl.BlockSpec((seq, w), lambda b, h: (b, h))] + round_out,
        out_shape=[jax.ShapeDtypeStruct((t, DIFF_HEADS * w), BF16)] + round_shapes,
        compiler_params=_params(2),
        name="diff_attn",
    )(zqkv, zqkv, zqkv, lq1, lk1, lq2, lk2, subln, *[job[0] for job in round_jobs])
    return res[0], res[1:]


def _merge_kernel(ya_ref, cv_ref, h_ref, wa_ref, wc_ref, wga_ref, wgc_ref, b_ref, o_ref, *, ts):
    b = b_ref[...]
    wa, wc = wa_ref[...].astype(BF16), wc_ref[...].astype(BF16)
    wga, wgc = wga_ref[...].astype(BF16), wgc_ref[...].astype(BF16)
    for r0 in range(0, h_ref.shape[0], ts):
        h = h_ref[r0:r0 + ts, :]
        ga = _sigmoid(_dot(h, wga) + b[0:1, :])
        gc = _sigmoid(_dot(h, wgc) + b[1:2, :])
        m = ga * _dot(ya_ref[r0:r0 + ts, :], wa) + gc * _dot(cv_ref[r0:r0 + ts, :], wc)
        o_ref[r0:r0 + ts, :] = m.astype(o_ref.dtype)


def _merge(ya, cv, h, wa, wc, wga, wgc, b_gates, *, tm, tn, ts):
    t, d = h.shape
    row = pl.BlockSpec((tm, d), lambda i, j: (i, 0))
    col = pl.BlockSpec((d, tn), lambda i, j: (0, j))
    return pl.pallas_call(
        functools.partial(_merge_kernel, ts=ts),
        grid=(t // tm, d // tn),
        in_specs=[row, row, row, col, col, col, col,
                  pl.BlockSpec((2, tn), lambda i, j: (0, j))],
        out_specs=pl.BlockSpec((tm, tn), lambda i, j: (i, j)),
        out_shape=jax.ShapeDtypeStruct((t, d), BF16),
        compiler_params=_params(2),
        name="gated_merge",
    )(ya, cv, h, wa, wc, wga, wgc, b_gates)


def _mix_out_kernel(m_ref, w_ref, x_ref, y_ref):
    y_ref[...] = x_ref[...] + _dot(m_ref[...], w_ref[...].astype(BF16))


def _mix_out(m, w, x, *, tm):
    t, d = x.shape
    row = pl.BlockSpec((tm, d), lambda i: (i, 0))
    return pl.pallas_call(
        _mix_out_kernel,
        grid=(t // tm,),
        in_specs=[row, pl.BlockSpec((d, d), lambda i: (0, 0)), row],
        out_specs=row,
        out_shape=jax.ShapeDtypeStruct((t, d), F32),
        compiler_params=_params(1),
        name="mix_out",
    )(m, w, x)


def _mem_kv_kernel(m_ref, g_ref, w_ref, o_ref, wb_ref):
    _cast_weights_once(pl.program_id(0), [(w_ref, wb_ref)])
    hm = _rms(m_ref[...], g_ref[...], NORM_EPS).astype(BF16)
    o_ref[...] = _dot(hm, wb_ref[...]).astype(o_ref.dtype)


def _mem_kv(mem, g, w, *, tm):
    t, d = mem.shape
    n = w.shape[1]
    return pl.pallas_call(
        _mem_kv_kernel,
        grid=(t // tm,),
        in_specs=[pl.BlockSpec((tm, d), lambda i: (i, 0)),
                  pl.BlockSpec((1, d), lambda i: (0, 0)),
                  pl.BlockSpec((d, n), lambda i: (0, 0))],
        out_specs=pl.BlockSpec((tm, n), lambda i: (i, 0)),
        out_shape=jax.ShapeDtypeStruct((t, n), BF16),
        scratch_shapes=[pltpu.VMEM((d, n), BF16)],
        compiler_params=_params(1),
        name="mem_kv",
    )(mem, g, w)


def _xattn_kernel(x_ref, g_ref, wq_ref, kv_ref, wo_ref, g2_ref, y_ref, n_ref, wqb_ref, wob_ref,
                  *, ts):
    _cast_weights_once(pl.program_id(0), [(wq_ref, wqb_ref), (wo_ref, wob_ref)])
    hd = XATTN_HEAD_DIM
    g, g2 = g_ref[...], g2_ref[...]
    kv = kv_ref[...]
    for r0 in range(0, x_ref.shape[0], ts):
        x = x_ref[r0:r0 + ts, :]
        n = _rms(x, g, NORM_EPS).astype(BF16)
        q = (_dot(n, wqb_ref[...]) * (hd ** -0.5)).astype(BF16)
        outs = []
        for h in range(XATTN_HEADS):
            k = kv[:, h * hd:(h + 1) * hd]
            v = kv[:, XATTN_WIDTH + h * hd:XATTN_WIDTH + (h + 1) * hd]
            s = _dot_nt(q[:, h * hd:(h + 1) * hd], k)
            p = jnp.exp(s - jnp.max(s, axis=-1, keepdims=True))
            l = jnp.sum(p, axis=-1, keepdims=True)
            outs.append((_dot(p.astype(BF16), v) / l).astype(BF16))
        o = jnp.concatenate(outs, axis=-1)
        y = x + _dot(o, wob_ref[...])
        y_ref[r0:r0 + ts, :] = y
        n_ref[r0:r0 + ts, :] = _rms(y, g2, NORM_EPS).astype(n_ref.dtype)


def _xattn(x, g, wq, kv, wo, g2, *, seq, tm, ts):
    t, d = x.shape
    tiles_per_seq = seq // tm
    row = pl.BlockSpec((tm, d), lambda i: (i, 0))
    return pl.pallas_call(
        functools.partial(_xattn_kernel, ts=ts),
        grid=(t // tm,),
        in_specs=[row,
                  pl.BlockSpec((1, d), lambda i: (0, 0)),
                  pl.BlockSpec((d, XATTN_WIDTH), lambda i: (0, 0)),
                  pl.BlockSpec((MEM_LEN, 2 * XATTN_WIDTH), lambda i: (i // tiles_per_seq, 0)),
                  pl.BlockSpec((XATTN_WIDTH, d), lambda i: (0, 0)),
                  pl.BlockSpec((1, d), lambda i: (0, 0))],
        out_specs=(row, row),
        out_shape=(jax.ShapeDtypeStruct((t, d), F32), jax.ShapeDtypeStruct((t, d), BF16)),
        scratch_shapes=[pltpu.VMEM((d, XATTN_WIDTH), BF16), pltpu.VMEM((XATTN_WIDTH, d), BF16)],
        compiler_params=_params(1),
        name="xattn",
    )(x, g, wq, kv, wo, g2)


def kernel(x, mem, ffn1_norm, ffn1_w_gate, ffn1_w_up, ffn1_w_down, mix_norm, w_mix_in, b_gates,
           lambda_q1, lambda_k1, lambda_q2, lambda_k2, diff_subln, w_attn_out, conv_w, w_conv_out,
           w_mix_out, xattn_norm, mem_norm, w_xq, w_xkv, w_xo, ffn2_norm, ffn2_w_gate, ffn2_w_up,
           ffn2_w_down, final_norm):
    b, s, d = x.shape
    depth = ffn1_norm.shape[0]
    t = b * s

    xt = x.reshape(t, d)
    memt = mem.reshape(b * mem.shape[1], d)
    out = None
    for l in range(depth):
        last = l == depth - 1
        row = lambda a: a[l].reshape(1, -1)
        lam_init = 0.8 - 0.6 * math.exp(-0.3 * l)

        ffn_up = dict(tm=2048, tn=2 * V7X_MXU_DIM, ts=256, slab=V7X_LANES)
        act, wdb = _ffn_up(_norm(xt, row(ffn1_norm), tm=1024), ffn1_w_gate[l], ffn1_w_up[l],
                           ffn1_w_down[l], **ffn_up)
        x1, h = _ffn_down(act, wdb, xt, row(mix_norm), emit_y=True, tm=256, ts=256)

        w_in = w_mix_in[l]
        zqkv = _qkv(h, w_in, tm=2048, tn=1024)
        cv = _conv_branch(h, w_in, conv_w[l], seq=s, tm=1024, tn=512, ts=256)
        slab = d // (b * DIFF_HEADS)
        ga_blk, gc_blk = 6, 7
        ya, (wa, wc, wga, wgc, wmo) = _diff_attn(
            zqkv, row(lambda_q1), row(lambda_k1), row(lambda_q2), row(lambda_k2), row(diff_subln),
            [(w_attn_out[l], slab, d, 0), (w_conv_out[l], slab, d, 0),
             (w_in, slab, d, ga_blk), (w_in, slab, d, gc_blk), (w_mix_out[l], slab, d, 0)],
            batch=b, seq=s, tq=512, lam_init=lam_init)
        m = _merge(ya, cv, h, wa, wc, wga, wgc, b_gates[l], tm=1024, tn=512, ts=256)
        x2 = _mix_out(m, wmo, x1, tm=512)

        kv = _mem_kv(memt, row(mem_norm), w_xkv[l], tm=512)
        x3, h3 = _xattn(x2, row(xattn_norm), w_xq[l], kv, w_xo[l], row(ffn2_norm), seq=s, tm=512,
                        ts=512)

        g_next = final_norm.reshape(1, -1) if last else ffn1_norm[l + 1].reshape(1, -1)
        act, wdb = _ffn_up(h3, ffn2_w_gate[l], ffn2_w_up[l], ffn2_w_down[l], **ffn_up)
        res = _ffn_down(act, wdb, x3, g_next, emit_y=not last, tm=256, ts=256)
        if last:
            out = res
        else:
            xt = res[0]
    return out.reshape(b, s, d)
```

```python
import functools
import math

import jax
import jax.numpy as jnp
from jax import lax
from jax.experimental import pallas as pl
from jax.experimental.pallas import tpu as pltpu

D_MODEL = 2048
MEM_LEN = 256
DIFF_HEADS = 8
DIFF_HEAD_DIM = 128
DIFF_PAIR = 2 * DIFF_HEAD_DIM
CONV_K = 3
XATTN_HEADS = 4
XATTN_HEAD_DIM = 128
XATTN_WIDTH = XATTN_HEADS * XATTN_HEAD_DIM
NORM_EPS = 1e-6
SUBLN_EPS = 1e-5
LOG2_E = math.log2(math.e)

V7X_LANES = 128
V7X_SUBLANES = 8
V7X_MXU_DIM = 256
V7X_VMEM_LIMIT_BYTES = 56 * 1024 * 1024

BF16 = jnp.bfloat16
F32 = jnp.float32


def _params(n_axes):
    return pltpu.CompilerParams(
        dimension_semantics=("arbitrary",) * n_axes,
        vmem_limit_bytes=V7X_VMEM_LIMIT_BYTES)


def _dot(a, b):
    return jnp.dot(a, b, preferred_element_type=F32)


def _dot_nt(a, b):
    return lax.dot_general(a, b, (((1,), (1,)), ((), ())), preferred_element_type=F32)


def _rms(x, g, eps):
    return x * lax.rsqrt(jnp.mean(x * x, axis=-1, keepdims=True) + eps) * g


def _sigmoid(x):
    return 1.0 / (1.0 + jnp.exp(-x))


def _cast_weights_once(step, pairs):
    @pl.when(step == 0)
    def _():
        for src, dst in pairs:
            dst[...] = src[...].astype(dst.dtype)


def _norm_kernel(x_ref, g_ref, o_ref):
    o_ref[...] = _rms(x_ref[...], g_ref[...], NORM_EPS).astype(o_ref.dtype)


def _norm(x, g, *, tm):
    t, d = x.shape
    row = pl.BlockSpec((tm, d), lambda i: (i, 0))
    return pl.pallas_call(
        _norm_kernel,
        grid=(t // tm,),
        in_specs=[row, pl.BlockSpec((1, d), lambda i: (0, 0))],
        out_specs=row,
        out_shape=jax.ShapeDtypeStruct((t, d), BF16),
        compiler_params=_params(1),
        name="ffn_norm",
    )(x, g)


def _ffn_up_kernel(h_ref, wg_ref, wu_ref, wd_ref, act_ref, wdb_ref, wgb_ref, wub_ref, *, ts):
    _cast_weights_once(pl.program_id(1), [(wg_ref, wgb_ref), (wu_ref, wub_ref)])
    wdb_ref[...] = wd_ref[...].astype(wdb_ref.dtype)
    wg, wu = wgb_ref[...], wub_ref[...]
    for r0 in range(0, h_ref.shape[0], ts):
        h = h_ref[r0:r0 + ts, :]
        gate = _dot(h, wg)
        half_act = (0.5 * gate) * _sigmoid(gate) * _dot(h, wu)
        act_ref[r0:r0 + ts, :] = half_act.astype(act_ref.dtype)


def _ffn_up(h, wg, wu, wd, *, tm, tn, ts, slab):
    t, d = h.shape
    f = wg.shape[1]
    n_j, n_i = pl.cdiv(f, tn), t // tm
    n_slab = pl.cdiv(f, slab)
    assert n_slab <= n_j * n_i
    slab_idx = lambda j, i: (jnp.minimum(j * n_i + i, n_slab - 1), 0)
    col = pl.BlockSpec((d, tn), lambda j, i: (0, j))
    return pl.pallas_call(
        functools.partial(_ffn_up_kernel, ts=ts),
        grid=(n_j, n_i),
        in_specs=[pl.BlockSpec((tm, d), lambda j, i: (i, 0)), col, col,
                  pl.BlockSpec((slab, d), slab_idx)],
        out_specs=(pl.BlockSpec((tm, tn), lambda j, i: (i, j)),
                   pl.BlockSpec((slab, d), slab_idx)),
        out_shape=(jax.ShapeDtypeStruct((t, f), BF16), jax.ShapeDtypeStruct((f, d), BF16)),
        scratch_shapes=[pltpu.VMEM((d, tn), BF16)] * 2,
        compiler_params=_params(2),
        name="ffn_up",
    )(h, wg, wu, wd)


def _ffn_down_kernel(act_ref, wd_ref, x_ref, g2_ref, *out_refs, emit_y):
    y = x_ref[...] + _dot(act_ref[...], wd_ref[...])
    if emit_y:
        out_refs[0][...] = y
    out_refs[-1][...] = _rms(y, g2_ref[...], NORM_EPS).astype(out_refs[-1].dtype)


def _ffn_down(act, wd, x, g2, *, emit_y, tm):
    t, d = x.shape
    f = act.shape[1]
    row = pl.BlockSpec((tm, d), lambda i: (i, 0))
    in_specs = [pl.BlockSpec((tm, f), lambda i: (i, 0)),
                pl.BlockSpec((f, d), lambda i: (0, 0), pipeline_mode=pl.Buffered(1)),
                row, pl.BlockSpec((1, d), lambda i: (0, 0))]
    if emit_y:
        out_shape = (jax.ShapeDtypeStruct((t, d), F32), jax.ShapeDtypeStruct((t, d), BF16))
        out_specs = (row, row)
    else:
        out_shape = jax.ShapeDtypeStruct((t, d), F32)
        out_specs = row
    return pl.pallas_call(
        functools.partial(_ffn_down_kernel, emit_y=emit_y),
        grid=(t // tm,),
        in_specs=in_specs, out_specs=out_specs, out_shape=out_shape,
        compiler_params=_params(1),
        name="ffn_down_y" if emit_y else "ffn_down_final",
    )(act, wd, x, g2)


def _qkv_kernel(h_ref, w_ref, o_ref, wb_ref, *, q_blocks, scale):
    j = pl.program_id(0)
    _cast_weights_once(pl.program_id(1), [(w_ref, wb_ref)])
    acc = _dot(h_ref[...], wb_ref[...])
    acc = acc * jnp.where(j < q_blocks, scale, 1.0).astype(F32)
    o_ref[...] = acc.astype(o_ref.dtype)


def _qkv(h, w, *, tm, tn):
    t, d = h.shape
    n = 3 * d
    return pl.pallas_call(
        functools.partial(_qkv_kernel, q_blocks=d // tn, scale=DIFF_HEAD_DIM ** -0.5 * LOG2_E),
        grid=(n // tn, t // tm),
        in_specs=[pl.BlockSpec((tm, d), lambda j, i: (i, 0)),
                  pl.BlockSpec((d, tn), lambda j, i: (0, j))],
        out_specs=pl.BlockSpec((tm, tn), lambda j, i: (i, j)),
        out_shape=jax.ShapeDtypeStruct((t, n), BF16),
        scratch_shapes=[pltpu.VMEM((d, tn), BF16)],
        compiler_params=_params(2),
        name="qkv_proj",
    )(h, w)


def _conv_kernel(h_ref, wb_ref, wc_ref, wu_ref, cw_ref, o_ref, carry_ref, wbb_ref, wcb_ref, wub_ref,
                 *, tiles_per_seq, ts):
    i = pl.program_id(1)
    _cast_weights_once(i, [(wb_ref, wbb_ref), (wc_ref, wcb_ref), (wu_ref, wub_ref)])

    @pl.when(i % tiles_per_seq == 0)
    def _():
        carry_ref[...] = jnp.zeros_like(carry_ref)

    prev = carry_ref[...]
    row8 = lax.broadcasted_iota(jnp.int32, prev.shape, 0)
    cw = cw_ref[...]
    wb, wc, wu = wbb_ref[...], wcb_ref[...], wub_ref[...]

    for r0 in range(0, h_ref.shape[0], ts):
        h = h_ref[r0:r0 + ts, :]
        cu = _dot(h, wc) * _dot(h, wu)

        def shifted(s):
            rolled = pltpu.roll(cu, s, axis=0)
            head = jnp.where(row8 < s, pltpu.roll(prev, s, axis=0), rolled[:V7X_SUBLANES, :])
            return jnp.concatenate([head, rolled[V7X_SUBLANES:, :]], axis=0)

        y = cw[2:3, :] * cu + cw[1:2, :] * shifted(1) + cw[0:1, :] * shifted(2)
        o_ref[r0:r0 + ts, :] = (_dot(h, wb) * y).astype(o_ref.dtype)
        prev = cu[ts - V7X_SUBLANES:, :]
    carry_ref[...] = prev


def _conv_branch(h, w, conv_w, *, seq, tm, tn, ts):
    t, d = h.shape
    nb = d // tn
    wspec = lambda k: pl.BlockSpec((d, tn), lambda j, i: (0, k * nb + j))
    return pl.pallas_call(
        functools.partial(_conv_kernel, tiles_per_seq=seq // tm, ts=ts),
        grid=(nb, t // tm),
        in_specs=[pl.BlockSpec((tm, d), lambda j, i: (i, 0)),
                  wspec(3), wspec(4), wspec(5),
                  pl.BlockSpec((CONV_K, tn), lambda j, i: (0, j))],
        out_specs=pl.BlockSpec((tm, tn), lambda j, i: (i, j)),
        out_shape=jax.ShapeDtypeStruct((t, d), BF16),
        scratch_shapes=[pltpu.VMEM((V7X_SUBLANES, tn), F32)] + [pltpu.VMEM((d, tn), BF16)] * 3,
        compiler_params=_params(2),
        name="conv_branch",
    )(h, w, w, w, conv_w)


def _diff_attn_kernel(*refs, tq, nq, lam_init, n_round):
    q_ref, k_ref, v_ref, lq1_ref, lk1_ref, lq2_ref, lk2_ref, sg_ref = refs[:8]
    o_ref = refs[8 + n_round]
    for src_ref, dst_ref in zip(refs[8:8 + n_round], refs[9 + n_round:]):
        dst_ref[...] = src_ref[...].astype(dst_ref.dtype)

    hd = DIFF_HEAD_DIM
    lam = (jnp.exp(jnp.sum(lq1_ref[...] * lk1_ref[...], axis=-1, keepdims=True))
           - jnp.exp(jnp.sum(lq2_ref[...] * lk2_ref[...], axis=-1, keepdims=True))
           + lam_init)
    sg = sg_ref[...]
    keep = (lax.broadcasted_iota(jnp.int32, (tq, tq), 1)
            <= lax.broadcasted_iota(jnp.int32, (tq, tq), 0))

    def softmax_terms(qc, kc, r0):
        s = _dot_nt(qc, kc)
        diag = jnp.where(keep, s[:, r0:], -jnp.inf)
        s = jnp.concatenate([s[:, :r0], diag], axis=1) if r0 else diag
        p = jnp.exp2(s - jnp.max(s, axis=-1, keepdims=True))
        return p, jnp.sum(p, axis=-1, keepdims=True)

    for qi in range(nq):
        r0 = qi * tq
        q = q_ref[r0:r0 + tq, :]
        k = k_ref[0:r0 + tq, :]
        p1, l1 = softmax_terms(q[:, :hd], k[:, :hd], r0)
        p2, l2 = softmax_terms(q[:, hd:], k[:, hd:], r0)
        v = v_ref[0:r0 + tq, :]
        o = _dot(p1.astype(BF16), v) / l1 - _dot(p2.astype(BF16), v) * (lam / l2)
        o = _rms(o, sg, SUBLN_EPS) * (1.0 - lam_init)
        o_ref[r0:r0 + tq, :] = o.astype(o_ref.dtype)


def _diff_attn(zqkv, lq1, lk1, lq2, lk2, subln, round_jobs, *, batch, seq, tq, lam_init):
    t = zqkv.shape[0]
    w = DIFF_PAIR
    lvec = pl.BlockSpec((1, DIFF_HEAD_DIM), lambda b, h: (0, 0))
    round_in, round_out, round_shapes = [], [], []
    for a, rb, cw, cblk in round_jobs:
        n_slab = pl.cdiv(a.shape[0], rb)
        assert n_slab <= batch * DIFF_HEADS and rb % (2 * V7X_SUBLANES) == 0
        slab = lambda b, h, n_slab=n_slab: jnp.minimum(b * DIFF_HEADS + h, n_slab - 1)
        round_in.append(pl.BlockSpec((rb, cw), lambda b, h, slab=slab, cblk=cblk: (slab(b, h), cblk)))
        round_out.append(pl.BlockSpec((rb, cw), lambda b, h, slab=slab: (slab(b, h), 0)))
        round_shapes.append(jax.ShapeDtypeStruct((a.shape[0], cw), BF16))
    res = pl.pallas_call(
        functools.partial(_diff_attn_kernel, tq=tq, nq=seq // tq, lam_init=lam_init,
                          n_round=len(round_jobs)),
        grid=(batch, DIFF_HEADS),
        in_specs=[pl.BlockSpec((seq, w), lambda b, h: (b, h)),
                  pl.BlockSpec((seq, w), lambda b, h: (b, DIFF_HEADS + h)),
                  pl.BlockSpec((seq, w), lambda b, h: (b, 2 * DIFF_HEADS + h)),
                  lvec, lvec, lvec, lvec,
                  pl.BlockSpec((1, w), lambda b, h: (0, 0))] + round_in,
        out_specs=[pl.BlockSpec((seq, w), lambda b, h: (b, h))] + round_out,
        out_shape=[jax.ShapeDtypeStruct((t, DIFF_HEADS * w), BF16)] + round_shapes,
        compiler_params=_params(2),
        name="diff_attn",
    )(zqkv, zqkv, zqkv, lq1, lk1, lq2, lk2, subln, *[job[0] for job in round_jobs])
    return res[0], res[1:]


def _merge_kernel(ya_ref, cv_ref, h_ref, wa_ref, wc_ref, wga_ref, wgc_ref, b_ref, o_ref, *, ts):
    b = b_ref[...]
    wa, wc = wa_ref[...].astype(BF16), wc_ref[...].astype(BF16)
    wga, wgc = wga_ref[...].astype(BF16), wgc_ref[...].astype(BF16)
    for r0 in range(0, h_ref.shape[0], ts):
        h = h_ref[r0:r0 + ts, :]
        ga = _sigmoid(_dot(h, wga) + b[0:1, :])
        gc = _sigmoid(_dot(h, wgc) + b[1:2, :])
        m = ga * _dot(ya_ref[r0:r0 + ts, :], wa) + gc * _dot(cv_ref[r0:r0 + ts, :], wc)
        o_ref[r0:r0 + ts, :] = m.astype(o_ref.dtype)


def _merge(ya, cv, h, wa, wc, wga, wgc, b_gates, *, tm, tn, ts):
    t, d = h.shape
    row = pl.BlockSpec((tm, d), lambda i, j: (i, 0))
    col = pl.BlockSpec((d, tn), lambda i, j: (0, j))
    return pl.pallas_call(
        functools.partial(_merge_kernel, ts=ts),
        grid=(t // tm, d // tn),
        in_specs=[row, row, row, col, col, col, col,
                  pl.BlockSpec((2, tn), lambda i, j: (0, j))],
        out_specs=pl.BlockSpec((tm, tn), lambda i, j: (i, j)),
        out_shape=jax.ShapeDtypeStruct((t, d), BF16),
        compiler_params=_params(2),
        name="gated_merge",
    )(ya, cv, h, wa, wc, wga, wgc, b_gates)


def _mem_kv_kernel(m_ref, g_ref, w_ref, o_ref, wb_ref):
    _cast_weights_once(pl.program_id(0), [(w_ref, wb_ref)])
    hm = _rms(m_ref[...], g_ref[...], NORM_EPS).astype(BF16)
    o_ref[...] = _dot(hm, wb_ref[...]).astype(o_ref.dtype)


def _mem_kv(mem, g, w, *, tm):
    t, d = mem.shape
    n = w.shape[1]
    return pl.pallas_call(
        _mem_kv_kernel,
        grid=(t // tm,),
        in_specs=[pl.BlockSpec((tm, d), lambda i: (i, 0)),
                  pl.BlockSpec((1, d), lambda i: (0, 0)),
                  pl.BlockSpec((d, n), lambda i: (0, 0))],
        out_specs=pl.BlockSpec((tm, n), lambda i: (i, 0)),
        out_shape=jax.ShapeDtypeStruct((t, n), BF16),
        scratch_shapes=[pltpu.VMEM((d, n), BF16)],
        compiler_params=_params(1),
        name="mem_kv",
    )(mem, g, w)


def _mix_xattn_kernel(m_ref, wmo_ref, x_ref, g_ref, wq_ref, kv_ref, wo_ref, g2_ref, y_ref, n_ref,
                      *, ts):
    hd = XATTN_HEAD_DIM
    g, g2 = g_ref[...], g2_ref[...]
    kv = kv_ref[...]
    for r0 in range(0, x_ref.shape[0], ts):
        x2 = x_ref[r0:r0 + ts, :] + _dot(m_ref[r0:r0 + ts, :], wmo_ref[...])
        n = _rms(x2, g, NORM_EPS).astype(BF16)
        q = (_dot(n, wq_ref[...]) * (hd ** -0.5)).astype(BF16)
        outs = []
        for h in range(XATTN_HEADS):
            k = kv[:, h * hd:(h + 1) * hd]
            v = kv[:, XATTN_WIDTH + h * hd:XATTN_WIDTH + (h + 1) * hd]
            s = _dot_nt(q[:, h * hd:(h + 1) * hd], k)
            p = jnp.exp(s - jnp.max(s, axis=-1, keepdims=True))
            l = jnp.sum(p, axis=-1, keepdims=True)
            outs.append((_dot(p.astype(BF16), v) / l).astype(BF16))
        o = jnp.concatenate(outs, axis=-1)
        y = x2 + _dot(o, wo_ref[...])
        y_ref[r0:r0 + ts, :] = y
        n_ref[r0:r0 + ts, :] = _rms(y, g2, NORM_EPS).astype(n_ref.dtype)


def _mix_xattn(m, wmo, x, g, wq, kv, wo, g2, *, seq, tm, ts):
    t, d = x.shape
    tiles_per_seq = seq // tm
    row = pl.BlockSpec((tm, d), lambda i: (i, 0))
    vec = pl.BlockSpec((1, d), lambda i: (0, 0))
    whole = lambda a: pl.BlockSpec(a.shape, lambda i: (0, 0), pipeline_mode=pl.Buffered(1))
    return pl.pallas_call(
        functools.partial(_mix_xattn_kernel, ts=ts),
        grid=(t // tm,),
        in_specs=[row, whole(wmo), row, vec, whole(wq),
                  pl.BlockSpec((MEM_LEN, 2 * XATTN_WIDTH), lambda i: (i // tiles_per_seq, 0)),
                  whole(wo), vec],
        out_specs=(row, row),
        out_shape=(jax.ShapeDtypeStruct((t, d), F32), jax.ShapeDtypeStruct((t, d), BF16)),
        compiler_params=_params(1),
        name="mix_xattn",
    )(m, wmo, x, g, wq, kv, wo, g2)


def kernel(x, mem, ffn1_norm, ffn1_w_gate, ffn1_w_up, ffn1_w_down, mix_norm, w_mix_in, b_gates,
           lambda_q1, lambda_k1, lambda_q2, lambda_k2, diff_subln, w_attn_out, conv_w, w_conv_out,
           w_mix_out, xattn_norm, mem_norm, w_xq, w_xkv, w_xo, ffn2_norm, ffn2_w_gate, ffn2_w_up,
           ffn2_w_down, final_norm):
    b, s, d = x.shape
    depth = ffn1_norm.shape[0]
    t = b * s

    xt = x.reshape(t, d)
    memt = mem.reshape(b * mem.shape[1], d)
    out = None
    for l in range(depth):
        last = l == depth - 1
        row = lambda a: a[l].reshape(1, -1)
        lam_init = 0.8 - 0.6 * math.exp(-0.3 * l)

        ffn_up = dict(tm=2048, tn=2 * V7X_MXU_DIM, ts=256, slab=V7X_LANES)
        act, wdb = _ffn_up(_norm(xt, row(ffn1_norm), tm=1024), ffn1_w_gate[l], ffn1_w_up[l],
                           ffn1_w_down[l], **ffn_up)
        x1, h = _ffn_down(act, wdb, xt, row(mix_norm), emit_y=True, tm=256)

        w_in = w_mix_in[l]
        zqkv = _qkv(h, w_in, tm=2048, tn=1024)
        cv = _conv_branch(h, w_in, conv_w[l], seq=s, tm=1024, tn=512, ts=256)
        slab = d // (b * DIFF_HEADS)
        ga_blk, gc_blk = 6, 7
        ya, (wa, wc, wga, wgc, wmo, wxq, wxo) = _diff_attn(
            zqkv, row(lambda_q1), row(lambda_k1), row(lambda_q2), row(lambda_k2), row(diff_subln),
            [(w_attn_out[l], slab, d, 0), (w_conv_out[l], slab, d, 0),
             (w_in, slab, d, ga_blk), (w_in, slab, d, gc_blk), (w_mix_out[l], slab, d, 0),
             (w_xq[l], slab, XATTN_WIDTH, 0), (w_xo[l], XATTN_WIDTH // (b * DIFF_HEADS), d, 0)],
            batch=b, seq=s, tq=512, lam_init=lam_init)
        m = _merge(ya, cv, h, wa, wc, wga, wgc, b_gates[l], tm=1024, tn=512, ts=256)

        kv = _mem_kv(memt, row(mem_norm), w_xkv[l], tm=512)
        x3, h3 = _mix_xattn(m, wmo, x1, row(xattn_norm), wxq, kv, wxo, row(ffn2_norm),
                            seq=s, tm=512, ts=256)

        g_next = final_norm.reshape(1, -1) if last else ffn1_norm[l + 1].reshape(1, -1)
        act, wdb = _ffn_up(h3, ffn2_w_gate[l], ffn2_w_up[l], ffn2_w_down[l], **ffn_up)
        res = _ffn_down(act, wdb, x3, g_next, emit_y=not last, tm=256)
        if last:
            out = res
        else:
            xt = res[0]
    return out.reshape(b, s, d)
```

```python
import functools
import math

import jax
import jax.numpy as jnp
from jax import lax
from jax.experimental import pallas as pl
from jax.experimental.pallas import tpu as pltpu

D_MODEL = 2048
MEM_LEN = 256
DIFF_HEADS = 8
DIFF_HEAD_DIM = 128
DIFF_PAIR = 2 * DIFF_HEAD_DIM
CONV_K = 3
XATTN_HEADS = 4
XATTN_HEAD_DIM = 128
XATTN_WIDTH = XATTN_HEADS * XATTN_HEAD_DIM
NORM_EPS = 1e-6
SUBLN_EPS = 1e-5
LOG2_E = math.log2(math.e)

V7X_LANES = 128
V7X_SUBLANES = 8
V7X_MXU_DIM = 256
V7X_VMEM_LIMIT_BYTES = 56 * 1024 * 1024

BF16 = jnp.bfloat16
F32 = jnp.float32


def _params(n_axes):
    return pltpu.CompilerParams(
        dimension_semantics=("arbitrary",) * n_axes,
        vmem_limit_bytes=V7X_VMEM_LIMIT_BYTES)


def _dot(a, b):
    return jnp.dot(a, b, preferred_element_type=F32)


def _dot_nt(a, b):
    return lax.dot_general(a, b, (((1,), (1,)), ((), ())), preferred_element_type=F32)


def _rms(x, g, eps):
    return x * lax.rsqrt(jnp.mean(x * x, axis=-1, keepdims=True) + eps) * g


def _sigmoid(x):
    return 1.0 / (1.0 + jnp.exp(-x))


def _cast_weights_once(step, pairs):
    @pl.when(step == 0)
    def _():
        for src, dst in pairs:
            dst[...] = src[...].astype(dst.dtype)


def _norm_kernel(x_ref, g_ref, o_ref):
    o_ref[...] = _rms(x_ref[...], g_ref[...], NORM_EPS).astype(o_ref.dtype)


def _norm(x, g, *, tm):
    t, d = x.shape
    row = pl.BlockSpec((tm, d), lambda i: (i, 0))
    return pl.pallas_call(
        _norm_kernel,
        grid=(t // tm,),
        in_specs=[row, pl.BlockSpec((1, d), lambda i: (0, 0))],
        out_specs=row,
        out_shape=jax.ShapeDtypeStruct((t, d), BF16),
        compiler_params=_params(1),
        name="ffn_norm",
    )(x, g)


def _ffn_up_kernel(h_ref, wg_ref, wu_ref, wd_ref, act_ref, wdb_ref, wgb_ref, wub_ref, *, ts):
    _cast_weights_once(pl.program_id(1), [(wg_ref, wgb_ref), (wu_ref, wub_ref)])
    wdb_ref[...] = wd_ref[...].astype(wdb_ref.dtype)
    wg, wu = wgb_ref[...], wub_ref[...]
    for r0 in range(0, h_ref.shape[0], ts):
        h = h_ref[r0:r0 + ts, :]
        gate = _dot(h, wg)
        act_ref[r0:r0 + ts, :] = (gate * _sigmoid(gate) * _dot(h, wu)).astype(act_ref.dtype)


def _ffn_up(h, wg, wu, wd, *, tm, tn, ts, slab):
    t, d = h.shape
    f = wg.shape[1]
    n_j, n_i = pl.cdiv(f, tn), t // tm
    n_slab = pl.cdiv(f, slab)
    assert n_slab <= n_j * n_i
    slab_idx = lambda j, i: (jnp.minimum(j * n_i + i, n_slab - 1), 0)
    col = pl.BlockSpec((d, tn), lambda j, i: (0, j))
    return pl.pallas_call(
        functools.partial(_ffn_up_kernel, ts=ts),
        grid=(n_j, n_i),
        in_specs=[pl.BlockSpec((tm, d), lambda j, i: (i, 0)), col, col,
                  pl.BlockSpec((slab, d), slab_idx)],
        out_specs=(pl.BlockSpec((tm, tn), lambda j, i: (i, j)),
                   pl.BlockSpec((slab, d), slab_idx)),
        out_shape=(jax.ShapeDtypeStruct((t, f), BF16), jax.ShapeDtypeStruct((f, d), BF16)),
        scratch_shapes=[pltpu.VMEM((d, tn), BF16)] * 2,
        compiler_params=_params(2),
        name="ffn_up",
    )(h, wg, wu, wd)


def _ffn_down_kernel(act_ref, wd_ref, x_ref, g2_ref, *out_refs, emit_y):
    y = x_ref[...] + 0.5 * _dot(act_ref[...], wd_ref[...])
    if emit_y:
        out_refs[0][...] = y
    out_refs[-1][...] = _rms(y, g2_ref[...], NORM_EPS).astype(out_refs[-1].dtype)


def _ffn_down(act, wd, x, g2, *, emit_y, tm):
    t, d = x.shape
    f = act.shape[1]
    row = pl.BlockSpec((tm, d), lambda i: (i, 0))
    in_specs = [pl.BlockSpec((tm, f), lambda i: (i, 0)),
                pl.BlockSpec((f, d), lambda i: (0, 0), pipeline_mode=pl.Buffered(1)),
                row, pl.BlockSpec((1, d), lambda i: (0, 0))]
    if emit_y:
        out_shape = (jax.ShapeDtypeStruct((t, d), F32), jax.ShapeDtypeStruct((t, d), BF16))
        out_specs = (row, row)
    else:
        out_shape = jax.ShapeDtypeStruct((t, d), F32)
        out_specs = row
    return pl.pallas_call(
        functools.partial(_ffn_down_kernel, emit_y=emit_y),
        grid=(t // tm,),
        in_specs=in_specs, out_specs=out_specs, out_shape=out_shape,
        compiler_params=_params(1),
        name="ffn_down_y" if emit_y else "ffn_down_final",
    )(act, wd, x, g2)


def _qkv_kernel(h_ref, w_ref, o_ref, wb_ref, *, q_blocks, scale):
    j = pl.program_id(0)
    _cast_weights_once(pl.program_id(1), [(w_ref, wb_ref)])
    acc = _dot(h_ref[...], wb_ref[...])
    acc = acc * jnp.where(j < q_blocks, scale, 1.0).astype(F32)
    o_ref[...] = acc.astype(o_ref.dtype)


def _qkv(h, w, *, tm, tn):
    t, d = h.shape
    n = 3 * d
    return pl.pallas_call(
        functools.partial(_qkv_kernel, q_blocks=d // tn, scale=DIFF_HEAD_DIM ** -0.5 * LOG2_E),
        grid=(n // tn, t // tm),
        in_specs=[pl.BlockSpec((tm, d), lambda j, i: (i, 0)),
                  pl.BlockSpec((d, tn), lambda j, i: (0, j))],
        out_specs=pl.BlockSpec((tm, tn), lambda j, i: (i, j)),
        out_shape=jax.ShapeDtypeStruct((t, n), BF16),
        scratch_shapes=[pltpu.VMEM((d, tn), BF16)],
        compiler_params=_params(2),
        name="qkv_proj",
    )(h, w)


def _conv_kernel(h_ref, wb_ref, wc_ref, wu_ref, cw_ref, o_ref, carry_ref, wbb_ref, wcb_ref, wub_ref,
                 *, tiles_per_seq, ts):
    i = pl.program_id(1)
    _cast_weights_once(i, [(wb_ref, wbb_ref), (wc_ref, wcb_ref), (wu_ref, wub_ref)])

    @pl.when(i % tiles_per_seq == 0)
    def _():
        carry_ref[...] = jnp.zeros_like(carry_ref)

    prev = carry_ref[...]
    row8 = lax.broadcasted_iota(jnp.int32, prev.shape, 0)
    cw = cw_ref[...]
    wb, wc, wu = wbb_ref[...], wcb_ref[...], wub_ref[...]

    for r0 in range(0, h_ref.shape[0], ts):
        h = h_ref[r0:r0 + ts, :]
        cu = _dot(h, wc) * _dot(h, wu)

        def shifted(s):
            rolled = pltpu.roll(cu, s, axis=0)
            head = jnp.where(row8 < s, pltpu.roll(prev, s, axis=0), rolled[:V7X_SUBLANES, :])
            return jnp.concatenate([head, rolled[V7X_SUBLANES:, :]], axis=0)

        y = cw[2:3, :] * cu + cw[1:2, :] * shifted(1) + cw[0:1, :] * shifted(2)
        o_ref[r0:r0 + ts, :] = (_dot(h, wb) * y).astype(o_ref.dtype)
        prev = cu[ts - V7X_SUBLANES:, :]
    carry_ref[...] = prev


def _conv_branch(h, w, conv_w, *, seq, tm, tn, ts):
    t, d = h.shape
    nb = d // tn
    wspec = lambda k: pl.BlockSpec((d, tn), lambda j, i: (0, k * nb + j))
    return pl.pallas_call(
        functools.partial(_conv_kernel, tiles_per_seq=seq // tm, ts=ts),
        grid=(nb, t // tm),
        in_specs=[pl.BlockSpec((tm, d), lambda j, i: (i, 0)),
                  wspec(3), wspec(4), wspec(5),
                  pl.BlockSpec((CONV_K, tn), lambda j, i: (0, j))],
        out_specs=pl.BlockSpec((tm, tn), lambda j, i: (i, j)),
        out_shape=jax.ShapeDtypeStruct((t, d), BF16),
        scratch_shapes=[pltpu.VMEM((V7X_SUBLANES, tn), F32)] + [pltpu.VMEM((d, tn), BF16)] * 3,
        compiler_params=_params(2),
        name="conv_branch",
    )(h, w, w, w, conv_w)


def _diff_attn_kernel(*refs, tq, nq, lam_init, n_round):
    q_ref, k_ref, v_ref, lq1_ref, lk1_ref, lq2_ref, lk2_ref, sg_ref = refs[:8]
    o_ref = refs[8 + n_round]
    for src_ref, dst_ref in zip(refs[8:8 + n_round], refs[9 + n_round:]):
        dst_ref[...] = src_ref[...].astype(dst_ref.dtype)

    hd = DIFF_HEAD_DIM
    lam = (jnp.exp(jnp.sum(lq1_ref[...] * lk1_ref[...], axis=-1, keepdims=True))
           - jnp.exp(jnp.sum(lq2_ref[...] * lk2_ref[...], axis=-1, keepdims=True))
           + lam_init)
    sg = sg_ref[...]
    keep = (lax.broadcasted_iota(jnp.int32, (tq, tq), 1)
            <= lax.broadcasted_iota(jnp.int32, (tq, tq), 0))

    def softmax_terms(qc, kc, r0):
        s = _dot_nt(qc, kc)
        s_diag = s[:, r0:]
        m = jnp.max(jnp.where(keep, s_diag, -jnp.inf), axis=-1, keepdims=True)
        if r0:
            m = jnp.maximum(m, jnp.max(s[:, :r0], axis=-1, keepdims=True))
        p = jnp.where(keep, jnp.exp2(s_diag - m), 0.0)
        if r0:
            p = jnp.concatenate([jnp.exp2(s[:, :r0] - m), p], axis=1)
        return p, jnp.sum(p, axis=-1, keepdims=True)

    for qi in range(nq):
        r0 = qi * tq
        q = q_ref[r0:r0 + tq, :]
        k = k_ref[0:r0 + tq, :]
        p1, l1 = softmax_terms(q[:, :hd], k[:, :hd], r0)
        p2, l2 = softmax_terms(q[:, hd:], k[:, hd:], r0)
        v = v_ref[0:r0 + tq, :]
        o = _dot(p1.astype(BF16), v) / l1 - _dot(p2.astype(BF16), v) * (lam / l2)
        o = _rms(o, sg, SUBLN_EPS) * (1.0 - lam_init)
        o_ref[r0:r0 + tq, :] = o.astype(o_ref.dtype)


def _diff_attn(zqkv, lq1, lk1, lq2, lk2, subln, round_jobs, *, batch, seq, tq, lam_init):
    t = zqkv.shape[0]
    w = DIFF_PAIR
    lvec = pl.BlockSpec((1, DIFF_HEAD_DIM), lambda b, h: (0, 0))
    round_in, round_out, round_shapes = [], [], []
    for a, rb, cw, cblk in round_jobs:
        n_slab = pl.cdiv(a.shape[0], rb)
        assert n_slab <= batch * DIFF_HEADS and rb % (2 * V7X_SUBLANES) == 0
        slab = lambda b, h, n_slab=n_slab: jnp.minimum(b * DIFF_HEADS + h, n_slab - 1)
        round_in.append(pl.BlockSpec((rb, cw), lambda b, h, slab=slab, cblk=cblk: (slab(b, h), cblk)))
        round_out.append(pl.BlockSpec((rb, cw), lambda b, h, slab=slab: (slab(b, h), 0)))
        round_shapes.append(jax.ShapeDtypeStruct((a.shape[0], cw), BF16))
    res = pl.pallas_call(
        functools.partial(_diff_attn_kernel, tq=tq, nq=seq // tq, lam_init=lam_init,
                          n_round=len(round_jobs)),
        grid=(batch, DIFF_HEADS),
        in_specs=[pl.BlockSpec((seq, w), lambda b, h: (b, h)),
                  pl.BlockSpec((seq, w), lambda b, h: (b, DIFF_HEADS + h)),
                  pl.BlockSpec((seq, w), lambda b, h: (b, 2 * DIFF_HEADS + h)),
                  lvec, lvec, lvec, lvec,
                  pl.BlockSpec((1, w), lambda b, h: (0, 0))] + round_in,
        out_specs=[pl.BlockSpec((seq, w), lambda b, h: (b, h))] + round_out,
        out_shape=[jax.ShapeDtypeStruct((t, DIFF_HEADS * w), BF16)] + round_shapes,
        compiler_params=_params(2),
        name="diff_attn",
    )(zqkv, zqkv, zqkv, lq1, lk1, lq2, lk2, subln, *[job[0] for job in round_jobs])
    return res[0], res[1:]


def _merge_kernel(ya_ref, cv_ref, h_ref, wa_ref, wc_ref, wga_ref, wgc_ref, b_ref, o_ref, *, ts):
    b = b_ref[...]
    wa, wc = wa_ref[...].astype(BF16), wc_ref[...].astype(BF16)
    wga, wgc = wga_ref[...].astype(BF16), wgc_ref[...].astype(BF16)
    for r0 in range(0, h_ref.shape[0], ts):
        h = h_ref[r0:r0 + ts, :]
        ga = _sigmoid(_dot(h, wga) + b[0:1, :])
        gc = _sigmoid(_dot(h, wgc) + b[1:2, :])
        m = ga * _dot(ya_ref[r0:r0 + ts, :], wa) + gc * _dot(cv_ref[r0:r0 + ts, :], wc)
        o_ref[r0:r0 + ts, :] = m.astype(o_ref.dtype)


def _merge(ya, cv, h, wa, wc, wga, wgc, b_gates, *, tm, tn, ts):
    t, d = h.shape
    row = pl.BlockSpec((tm, d), lambda i, j: (i, 0))
    col = pl.BlockSpec((d, tn), lambda i, j: (0, j))
    return pl.pallas_call(
        functools.partial(_merge_kernel, ts=ts),
        grid=(t // tm, d // tn),
        in_specs=[row, row, row, col, col, col, col,
                  pl.BlockSpec((2, tn), lambda i, j: (0, j))],
        out_specs=pl.BlockSpec((tm, tn), lambda i, j: (i, j)),
        out_shape=jax.ShapeDtypeStruct((t, d), BF16),
        compiler_params=_params(2),
        name="gated_merge",
    )(ya, cv, h, wa, wc, wga, wgc, b_gates)


def _mem_kv_kernel(m_ref, g_ref, w_ref, o_ref, wb_ref):
    _cast_weights_once(pl.program_id(0), [(w_ref, wb_ref)])
    hm = _rms(m_ref[...], g_ref[...], NORM_EPS).astype(BF16)
    o_ref[...] = _dot(hm, wb_ref[...]).astype(o_ref.dtype)


def _mem_kv(mem, g, w, *, tm):
    t, d = mem.shape
    n = w.shape[1]
    return pl.pallas_call(
        _mem_kv_kernel,
        grid=(t // tm,),
        in_specs=[pl.BlockSpec((tm, d), lambda i: (i, 0)),
                  pl.BlockSpec((1, d), lambda i: (0, 0)),
                  pl.BlockSpec((d, n), lambda i: (0, 0))],
        out_specs=pl.BlockSpec((tm, n), lambda i: (i, 0)),
        out_shape=jax.ShapeDtypeStruct((t, n), BF16),
        scratch_shapes=[pltpu.VMEM((d, n), BF16)],
        compiler_params=_params(1),
        name="mem_kv",
    )(mem, g, w)


def _mix_xattn_kernel(m_ref, wmo_ref, x_ref, g_ref, wq_ref, kv_ref, wo_ref, g2_ref, y_ref, n_ref,
                      *, ts):
    hd = XATTN_HEAD_DIM
    g, g2 = g_ref[...], g2_ref[...]
    kv = kv_ref[...]
    subs = range(0, x_ref.shape[0], ts)
    x2 = [x_ref[r0:r0 + ts, :] + _dot(m_ref[r0:r0 + ts, :], wmo_ref[...]) for r0 in subs]
    n = [_rms(a, g, NORM_EPS).astype(BF16) for a in x2]
    q = [(_dot(a, wq_ref[...]) * (hd ** -0.5)).astype(BF16) for a in n]
    outs = [[] for _ in subs]
    for h in range(XATTN_HEADS):
        k = kv[:, h * hd:(h + 1) * hd]
        v = kv[:, XATTN_WIDTH + h * hd:XATTN_WIDTH + (h + 1) * hd]
        s = [_dot_nt(a[:, h * hd:(h + 1) * hd], k) for a in q]
        p = [jnp.exp(a - jnp.max(a, axis=-1, keepdims=True)) for a in s]
        for i, a in enumerate(p):
            l = jnp.sum(a, axis=-1, keepdims=True)
            outs[i].append((_dot(a.astype(BF16), v) / l).astype(BF16))
    o = [jnp.concatenate(a, axis=-1) for a in outs]
    y = [a + _dot(b, wo_ref[...]) for a, b in zip(x2, o)]
    for r0, a in zip(subs, y):
        y_ref[r0:r0 + ts, :] = a
        n_ref[r0:r0 + ts, :] = _rms(a, g2, NORM_EPS).astype(n_ref.dtype)


def _mix_xattn(m, wmo, x, g, wq, kv, wo, g2, *, seq, tm, ts):
    t, d = x.shape
    tiles_per_seq = seq // tm
    row = pl.BlockSpec((tm, d), lambda i: (i, 0))
    vec = pl.BlockSpec((1, d), lambda i: (0, 0))
    whole = lambda a: pl.BlockSpec(a.shape, lambda i: (0, 0), pipeline_mode=pl.Buffered(1))
    return pl.pallas_call(
        functools.partial(_mix_xattn_kernel, ts=ts),
        grid=(t // tm,),
        in_specs=[row, whole(wmo), row, vec, whole(wq),
                  pl.BlockSpec((MEM_LEN, 2 * XATTN_WIDTH), lambda i: (i // tiles_per_seq, 0)),
                  whole(wo), vec],
        out_specs=(row, row),
        out_shape=(jax.ShapeDtypeStruct((t, d), F32), jax.ShapeDtypeStruct((t, d), BF16)),
        compiler_params=_params(1),
        name="mix_xattn",
    )(m, wmo, x, g, wq, kv, wo, g2)


def kernel(x, mem, ffn1_norm, ffn1_w_gate, ffn1_w_up, ffn1_w_down, mix_norm, w_mix_in, b_gates,
           lambda_q1, lambda_k1, lambda_q2, lambda_k2, diff_subln, w_attn_out, conv_w, w_conv_out,
           w_mix_out, xattn_norm, mem_norm, w_xq, w_xkv, w_xo, ffn2_norm, ffn2_w_gate, ffn2_w_up,
           ffn2_w_down, final_norm):
    b, s, d = x.shape
    depth = ffn1_norm.shape[0]
    t = b * s

    xt = x.reshape(t, d)
    memt = mem.reshape(b * mem.shape[1], d)
    out = None
    for l in range(depth):
        last = l == depth - 1
        row = lambda a: a[l].reshape(1, -1)
        lam_init = 0.8 - 0.6 * math.exp(-0.3 * l)

        ffn_up = dict(tm=2048, tn=2 * V7X_MXU_DIM, ts=256, slab=V7X_LANES)
        act, wdb = _ffn_up(_norm(xt, row(ffn1_norm), tm=1024), ffn1_w_gate[l], ffn1_w_up[l],
                           ffn1_w_down[l], **ffn_up)
        x1, h = _ffn_down(act, wdb, xt, row(mix_norm), emit_y=True, tm=256)

        w_in = w_mix_in[l]
        zqkv = _qkv(h, w_in, tm=2048, tn=1024)
        cv = _conv_branch(h, w_in, conv_w[l], seq=s, tm=1024, tn=512, ts=256)
        slab = d // (b * DIFF_HEADS)
        ga_blk, gc_blk = 6, 7
        ya, (wa, wc, wga, wgc, wmo, wxq, wxo) = _diff_attn(
            zqkv, row(lambda_q1), row(lambda_k1), row(lambda_q2), row(lambda_k2), row(diff_subln),
            [(w_attn_out[l], slab, d, 0), (w_conv_out[l], slab, d, 0),
             (w_in, slab, d, ga_blk), (w_in, slab, d, gc_blk), (w_mix_out[l], slab, d, 0),
             (w_xq[l], slab, XATTN_WIDTH, 0), (w_xo[l], XATTN_WIDTH // (b * DIFF_HEADS), d, 0)],
            batch=b, seq=s, tq=512, lam_init=lam_init)
        m = _merge(ya, cv, h, wa, wc, wga, wgc, b_gates[l], tm=1024, tn=512, ts=256)

        kv = _mem_kv(memt, row(mem_norm), w_xkv[l], tm=512)
        x3, h3 = _mix_xattn(m, wmo, x1, row(xattn_norm), wxq, kv, wxo, row(ffn2_norm),
                            seq=s, tm=512, ts=256)

        g_next = final_norm.reshape(1, -1) if last else ffn1_norm[l + 1].reshape(1, -1)
        act, wdb = _ffn_up(h3, ffn2_w_gate[l], ffn2_w_up[l], ffn2_w_down[l], **ffn_up)
        res = _ffn_down(act, wdb, x3, g_next, emit_y=not last, tm=256)
        if last:
            out = res
        else:
            xt = res[0]
    return out.reshape(b, s, d)
```

```python
import functools
import math

import jax
import jax.numpy as jnp
from jax import lax
from jax.experimental import pallas as pl
from jax.experimental.pallas import tpu as pltpu

D_MODEL = 2048
MEM_LEN = 256
DIFF_HEADS = 8
DIFF_HEAD_DIM = 128
DIFF_PAIR = 2 * DIFF_HEAD_DIM
CONV_K = 3
XATTN_HEADS = 4
XATTN_HEAD_DIM = 128
XATTN_WIDTH = XATTN_HEADS * XATTN_HEAD_DIM
NORM_EPS = 1e-6
SUBLN_EPS = 1e-5
LOG2_E = math.log2(math.e)
SCORE_TILES_AHEAD = 2

V7X_LANES = 128
V7X_SUBLANES = 8
V7X_MXU_DIM = 256
V7X_VMEM_LIMIT_BYTES = 56 * 1024 * 1024

BF16 = jnp.bfloat16
F32 = jnp.float32


def _params(n_axes):
    return pltpu.CompilerParams(
        dimension_semantics=("arbitrary",) * n_axes,
        vmem_limit_bytes=V7X_VMEM_LIMIT_BYTES)


def _dot(a, b):
    return jnp.dot(a, b, preferred_element_type=F32)


def _dot_nt(a, b):
    return lax.dot_general(a, b, (((1,), (1,)), ((), ())), preferred_element_type=F32)


def _rms(x, g, eps):
    return x * lax.rsqrt(jnp.mean(x * x, axis=-1, keepdims=True) + eps) * g


def _sigmoid(x):
    return 1.0 / (1.0 + jnp.exp(-x))


def _cast_weights_once(step, pairs):
    @pl.when(step == 0)
    def _():
        for src, dst in pairs:
            dst[...] = src[...].astype(dst.dtype)


def _norm_kernel(x_ref, g_ref, o_ref):
    o_ref[...] = _rms(x_ref[...], g_ref[...], NORM_EPS).astype(o_ref.dtype)


def _norm(x, g, *, tm):
    t, d = x.shape
    row = pl.BlockSpec((tm, d), lambda i: (i, 0))
    return pl.pallas_call(
        _norm_kernel,
        grid=(t // tm,),
        in_specs=[row, pl.BlockSpec((1, d), lambda i: (0, 0))],
        out_specs=row,
        out_shape=jax.ShapeDtypeStruct((t, d), BF16),
        compiler_params=_params(1),
        name="ffn_norm",
    )(x, g)


def _ffn_up_kernel(h_ref, wg_ref, wu_ref, wd_ref, act_ref, wdb_ref, wgb_ref, wub_ref, *, ts):
    _cast_weights_once(pl.program_id(1), [(wg_ref, wgb_ref), (wu_ref, wub_ref)])
    wdb_ref[...] = wd_ref[...].astype(wdb_ref.dtype)
    wg, wu = wgb_ref[...], wub_ref[...]
    for r0 in range(0, h_ref.shape[0], ts):
        h = h_ref[r0:r0 + ts, :]
        gate = _dot(h, wg)
        act_ref[r0:r0 + ts, :] = (gate * _sigmoid(gate) * _dot(h, wu)).astype(act_ref.dtype)


def _ffn_up(h, wg, wu, wd, *, tm, tn, ts, slab):
    t, d = h.shape
    f = wg.shape[1]
    n_j, n_i = pl.cdiv(f, tn), t // tm
    n_slab = pl.cdiv(f, slab)
    assert n_slab <= n_j * n_i
    slab_idx = lambda j, i: (jnp.minimum(j * n_i + i, n_slab - 1), 0)
    col = pl.BlockSpec((d, tn), lambda j, i: (0, j))
    return pl.pallas_call(
        functools.partial(_ffn_up_kernel, ts=ts),
        grid=(n_j, n_i),
        in_specs=[pl.BlockSpec((tm, d), lambda j, i: (i, 0)), col, col,
                  pl.BlockSpec((slab, d), slab_idx)],
        out_specs=(pl.BlockSpec((tm, tn), lambda j, i: (i, j)),
                   pl.BlockSpec((slab, d), slab_idx)),
        out_shape=(jax.ShapeDtypeStruct((t, f), BF16), jax.ShapeDtypeStruct((f, d), BF16)),
        scratch_shapes=[pltpu.VMEM((d, tn), BF16)] * 2,
        compiler_params=_params(2),
        name="ffn_up",
    )(h, wg, wu, wd)


def _ffn_down_kernel(act_ref, wd_ref, x_ref, g2_ref, *out_refs, emit_y):
    y = x_ref[...] + 0.5 * _dot(act_ref[...], wd_ref[...])
    if emit_y:
        out_refs[0][...] = y
    out_refs[-1][...] = _rms(y, g2_ref[...], NORM_EPS).astype(out_refs[-1].dtype)


def _ffn_down(act, wd, x, g2, *, emit_y, tm):
    t, d = x.shape
    f = act.shape[1]
    row = pl.BlockSpec((tm, d), lambda i: (i, 0))
    in_specs = [pl.BlockSpec((tm, f), lambda i: (i, 0)),
                pl.BlockSpec((f, d), lambda i: (0, 0), pipeline_mode=pl.Buffered(1)),
                row, pl.BlockSpec((1, d), lambda i: (0, 0))]
    if emit_y:
        out_shape = (jax.ShapeDtypeStruct((t, d), F32), jax.ShapeDtypeStruct((t, d), BF16))
        out_specs = (row, row)
    else:
        out_shape = jax.ShapeDtypeStruct((t, d), F32)
        out_specs = row
    return pl.pallas_call(
        functools.partial(_ffn_down_kernel, emit_y=emit_y),
        grid=(t // tm,),
        in_specs=in_specs, out_specs=out_specs, out_shape=out_shape,
        compiler_params=_params(1),
        name="ffn_down_y" if emit_y else "ffn_down_final",
    )(act, wd, x, g2)


def _qkv_kernel(h_ref, w_ref, o_ref, wb_ref, *, q_blocks, scale):
    j = pl.program_id(0)
    _cast_weights_once(pl.program_id(1), [(w_ref, wb_ref)])
    acc = _dot(h_ref[...], wb_ref[...])
    acc = acc * jnp.where(j < q_blocks, scale, 1.0).astype(F32)
    o_ref[...] = acc.astype(o_ref.dtype)


def _qkv(h, w, *, tm, tn):
    t, d = h.shape
    n = 3 * d
    return pl.pallas_call(
        functools.partial(_qkv_kernel, q_blocks=d // tn, scale=DIFF_HEAD_DIM ** -0.5 * LOG2_E),
        grid=(n // tn, t // tm),
        in_specs=[pl.BlockSpec((tm, d), lambda j, i: (i, 0)),
                  pl.BlockSpec((d, tn), lambda j, i: (0, j))],
        out_specs=pl.BlockSpec((tm, tn), lambda j, i: (i, j)),
        out_shape=jax.ShapeDtypeStruct((t, n), BF16),
        scratch_shapes=[pltpu.VMEM((d, tn), BF16)],
        compiler_params=_params(2),
        name="qkv_proj",
    )(h, w)


def _conv_kernel(h_ref, wb_ref, wc_ref, wu_ref, cw_ref, o_ref, carry_ref, wbb_ref, wcb_ref, wub_ref,
                 *, tiles_per_seq, ts):
    i = pl.program_id(1)
    _cast_weights_once(i, [(wb_ref, wbb_ref), (wc_ref, wcb_ref), (wu_ref, wub_ref)])

    @pl.when(i % tiles_per_seq == 0)
    def _():
        carry_ref[...] = jnp.zeros_like(carry_ref)

    prev = carry_ref[...]
    row8 = lax.broadcasted_iota(jnp.int32, prev.shape, 0)
    cw = cw_ref[...]
    wb, wc, wu = wbb_ref[...], wcb_ref[...], wub_ref[...]

    for r0 in range(0, h_ref.shape[0], ts):
        h = h_ref[r0:r0 + ts, :]
        cu = _dot(h, wc) * _dot(h, wu)

        def shifted(s):
            rolled = pltpu.roll(cu, s, axis=0)
            head = jnp.where(row8 < s, pltpu.roll(prev, s, axis=0), rolled[:V7X_SUBLANES, :])
            return jnp.concatenate([head, rolled[V7X_SUBLANES:, :]], axis=0)

        y = cw[2:3, :] * cu + cw[1:2, :] * shifted(1) + cw[0:1, :] * shifted(2)
        o_ref[r0:r0 + ts, :] = (_dot(h, wb) * y).astype(o_ref.dtype)
        prev = cu[ts - V7X_SUBLANES:, :]
    carry_ref[...] = prev


def _conv_branch(h, w, conv_w, *, seq, tm, tn, ts):
    t, d = h.shape
    nb = d // tn
    wspec = lambda k: pl.BlockSpec((d, tn), lambda j, i: (0, k * nb + j))
    return pl.pallas_call(
        functools.partial(_conv_kernel, tiles_per_seq=seq // tm, ts=ts),
        grid=(nb, t // tm),
        in_specs=[pl.BlockSpec((tm, d), lambda j, i: (i, 0)),
                  wspec(3), wspec(4), wspec(5),
                  pl.BlockSpec((CONV_K, tn), lambda j, i: (0, j))],
        out_specs=pl.BlockSpec((tm, tn), lambda j, i: (i, j)),
        out_shape=jax.ShapeDtypeStruct((t, d), BF16),
        scratch_shapes=[pltpu.VMEM((V7X_SUBLANES, tn), F32)] + [pltpu.VMEM((d, tn), BF16)] * 3,
        compiler_params=_params(2),
        name="conv_branch",
    )(h, w, w, w, conv_w)


def _diff_attn_kernel(*refs, tq, nq, lam_init, n_round):
    q_ref, k_ref, v_ref, lq1_ref, lk1_ref, lq2_ref, lk2_ref, sg_ref = refs[:8]
    o_ref = refs[8 + n_round]
    for src_ref, dst_ref in zip(refs[8:8 + n_round], refs[9 + n_round:]):
        dst_ref[...] = src_ref[...].astype(dst_ref.dtype)

    hd = DIFF_HEAD_DIM
    lam = (jnp.exp(jnp.sum(lq1_ref[...] * lk1_ref[...], axis=-1, keepdims=True))
           - jnp.exp(jnp.sum(lq2_ref[...] * lk2_ref[...], axis=-1, keepdims=True))
           + lam_init)
    sg = sg_ref[...]
    keep = (lax.broadcasted_iota(jnp.int32, (tq, tq), 1)
            <= lax.broadcasted_iota(jnp.int32, (tq, tq), 0))

    def scores(qi):
        q = q_ref[qi * tq:(qi + 1) * tq, :]
        k = k_ref[0:(qi + 1) * tq, :]
        return _dot_nt(q[:, :hd], k[:, :hd]), _dot_nt(q[:, hd:], k[:, hd:])

    def softmax_terms(s, r0):
        s_diag = s[:, r0:]
        m = jnp.max(jnp.where(keep, s_diag, -jnp.inf), axis=-1, keepdims=True)
        if r0:
            m = jnp.maximum(m, jnp.max(s[:, :r0], axis=-1, keepdims=True))
        p = jnp.where(keep, jnp.exp2(s_diag - m), 0.0)
        if r0:
            p = jnp.concatenate([jnp.exp2(s[:, :r0] - m), p], axis=1)
        return p, jnp.sum(p, axis=-1, keepdims=True)

    order = list(reversed(range(nq)))
    ahead = SCORE_TILES_AHEAD
    queue = [scores(qi) for qi in order[:ahead]]
    for pos, qi in enumerate(order):
        r0 = qi * tq
        s1, s2 = queue.pop(0)
        if pos + ahead < nq:
            queue.append(scores(order[pos + ahead]))
        p1, l1 = softmax_terms(s1, r0)
        p2, l2 = softmax_terms(s2, r0)
        v = v_ref[0:r0 + tq, :]
        a = p1 - p2 * (lam * l1 / l2)
        o = _dot(a.astype(BF16), v) / l1
        o = _rms(o, sg, SUBLN_EPS) * (1.0 - lam_init)
        o_ref[r0:r0 + tq, :] = o.astype(o_ref.dtype)


def _diff_attn(zqkv, lq1, lk1, lq2, lk2, subln, round_jobs, *, batch, seq, tq, lam_init):
    t = zqkv.shape[0]
    w = DIFF_PAIR
    lvec = pl.BlockSpec((1, DIFF_HEAD_DIM), lambda b, h: (0, 0))
    round_in, round_out, round_shapes = [], [], []
    for a, rb, cw, cblk in round_jobs:
        n_slab = pl.cdiv(a.shape[0], rb)
        assert n_slab <= batch * DIFF_HEADS and rb % (2 * V7X_SUBLANES) == 0
        slab = lambda b, h, n_slab=n_slab: jnp.minimum(b * DIFF_HEADS + h, n_slab - 1)
        round_in.append(pl.BlockSpec((rb, cw), lambda b, h, slab=slab, cblk=cblk: (slab(b, h), cblk)))
        round_out.append(pl.BlockSpec((rb, cw), lambda b, h, slab=slab: (slab(b, h), 0)))
        round_shapes.append(jax.ShapeDtypeStruct((a.shape[0], cw), BF16))
    res = pl.pallas_call(
        functools.partial(_diff_attn_kernel, tq=tq, nq=seq // tq, lam_init=lam_init,
                          n_round=len(round_jobs)),
        grid=(batch, DIFF_HEADS),
        in_specs=[pl.BlockSpec((seq, w), lambda b, h: (b, h)),
                  pl.BlockSpec((seq, w), lambda b, h: (b, DIFF_HEADS + h)),
                  pl.BlockSpec((seq, w), lambda b, h: (b, 2 * DIFF_HEADS + h)),
                  lvec, lvec, lvec, lvec,
                  pl.BlockSpec((1, w), lambda b, h: (0, 0))] + round_in,
        out_specs=[pl.BlockSpec((seq, w), lambda b, h: (b, h))] + round_out,
        out_shape=[jax.ShapeDtypeStruct((t, DIFF_HEADS * w), BF16)] + round_shapes,
        compiler_params=_params(2),
        name="diff_attn",
    )(zqkv, zqkv, zqkv, lq1, lk1, lq2, lk2, subln, *[job[0] for job in round_jobs])
    return res[0], res[1:]


def _merge_kernel(ya_ref, cv_ref, h_ref, wa_ref, wc_ref, wga_ref, wgc_ref, b_ref, o_ref, *, ts):
    b = b_ref[...]
    wa, wc = wa_ref[...].astype(BF16), wc_ref[...].astype(BF16)
    wga, wgc = wga_ref[...].astype(BF16), wgc_ref[...].astype(BF16)
    for r0 in range(0, h_ref.shape[0], ts):
        h = h_ref[r0:r0 + ts, :]
        ga = _sigmoid(_dot(h, wga) + b[0:1, :])
        gc = _sigmoid(_dot(h, wgc) + b[1:2, :])
        m = ga * _dot(ya_ref[r0:r0 + ts, :], wa) + gc * _dot(cv_ref[r0:r0 + ts, :], wc)
        o_ref[r0:r0 + ts, :] = m.astype(o_ref.dtype)


def _merge(ya, cv, h, wa, wc, wga, wgc, b_gates, *, tm, tn, ts):
    t, d = h.shape
    row = pl.BlockSpec((tm, d), lambda i, j: (i, 0))
    col = pl.BlockSpec((d, tn), lambda i, j: (0, j))
    return pl.pallas_call(
        functools.partial(_merge_kernel, ts=ts),
        grid=(t // tm, d // tn),
        in_specs=[row, row, row, col, col, col, col,
                  pl.BlockSpec((2, tn), lambda i, j: (0, j))],
        out_specs=pl.BlockSpec((tm, tn), lambda i, j: (i, j)),
        out_shape=jax.ShapeDtypeStruct((t, d), BF16),
        compiler_params=_params(2),
        name="gated_merge",
    )(ya, cv, h, wa, wc, wga, wgc, b_gates)


def _mem_kv_kernel(m_ref, g_ref, w_ref, o_ref, wb_ref):
    _cast_weights_once(pl.program_id(0), [(w_ref, wb_ref)])
    hm = _rms(m_ref[...], g_ref[...], NORM_EPS).astype(BF16)
    o_ref[...] = _dot(hm, wb_ref[...]).astype(o_ref.dtype)


def _mem_kv(mem, g, w, *, tm):
    t, d = mem.shape
    n = w.shape[1]
    return pl.pallas_call(
        _mem_kv_kernel,
        grid=(t // tm,),
        in_specs=[pl.BlockSpec((tm, d), lambda i: (i, 0)),
                  pl.BlockSpec((1, d), lambda i: (0, 0)),
                  pl.BlockSpec((d, n), lambda i: (0, 0))],
        out_specs=pl.BlockSpec((tm, n), lambda i: (i, 0)),
        out_shape=jax.ShapeDtypeStruct((t, n), BF16),
        scratch_shapes=[pltpu.VMEM((d, n), BF16)],
        compiler_params=_params(1),
        name="mem_kv",
    )(mem, g, w)


def _mix_xattn_kernel(m_ref, wmo_ref, x_ref, g_ref, wq_ref, kv_ref, wo_ref, g2_ref, y_ref, n_ref,
                      *, ts):
    hd = XATTN_HEAD_DIM
    g, g2 = g_ref[...], g2_ref[...]
    kv = kv_ref[...]
    subs = range(0, x_ref.shape[0], ts)
    x2 = [x_ref[r0:r0 + ts, :] + _dot(m_ref[r0:r0 + ts, :], wmo_ref[...]) for r0 in subs]
    n = [_rms(a, g, NORM_EPS).astype(BF16) for a in x2]
    q = [(_dot(a, wq_ref[...]) * (hd ** -0.5)).astype(BF16) for a in n]
    outs = [[] for _ in subs]
    for h in range(XATTN_HEADS):
        k = kv[:, h * hd:(h + 1) * hd]
        v = kv[:, XATTN_WIDTH + h * hd:XATTN_WIDTH + (h + 1) * hd]
        s = [_dot_nt(a[:, h * hd:(h + 1) * hd], k) for a in q]
        p = [jnp.exp(a - jnp.max(a, axis=-1, keepdims=True)) for a in s]
        for i, a in enumerate(p):
            l = jnp.sum(a, axis=-1, keepdims=True)
            outs[i].append((_dot(a.astype(BF16), v) / l).astype(BF16))
    o = [jnp.concatenate(a, axis=-1) for a in outs]
    y = [a + _dot(b, wo_ref[...]) for a, b in zip(x2, o)]
    for r0, a in zip(subs, y):
        y_ref[r0:r0 + ts, :] = a
        n_ref[r0:r0 + ts, :] = _rms(a, g2, NORM_EPS).astype(n_ref.dtype)


def _mix_xattn(m, wmo, x, g, wq, kv, wo, g2, *, seq, tm, ts):
    t, d = x.shape
    tiles_per_seq = seq // tm
    row = pl.BlockSpec((tm, d), lambda i: (i, 0))
    vec = pl.BlockSpec((1, d), lambda i: (0, 0))
    whole = lambda a: pl.BlockSpec(a.shape, lambda i: (0, 0), pipeline_mode=pl.Buffered(1))
    return pl.pallas_call(
        functools.partial(_mix_xattn_kernel, ts=ts),
        grid=(t // tm,),
        in_specs=[row, whole(wmo), row, vec, whole(wq),
                  pl.BlockSpec((MEM_LEN, 2 * XATTN_WIDTH), lambda i: (i // tiles_per_seq, 0)),
                  whole(wo), vec],
        out_specs=(row, row),
        out_shape=(jax.ShapeDtypeStruct((t, d), F32), jax.ShapeDtypeStruct((t, d), BF16)),
        compiler_params=_params(1),
        name="mix_xattn",
    )(m, wmo, x, g, wq, kv, wo, g2)


def kernel(x, mem, ffn1_norm, ffn1_w_gate, ffn1_w_up, ffn1_w_down, mix_norm, w_mix_in, b_gates,
           lambda_q1, lambda_k1, lambda_q2, lambda_k2, diff_subln, w_attn_out, conv_w, w_conv_out,
           w_mix_out, xattn_norm, mem_norm, w_xq, w_xkv, w_xo, ffn2_norm, ffn2_w_gate, ffn2_w_up,
           ffn2_w_down, final_norm):
    b, s, d = x.shape
    depth = ffn1_norm.shape[0]
    t = b * s

    xt = x.reshape(t, d)
    memt = mem.reshape(b * mem.shape[1], d)
    out = None
    for l in range(depth):
        last = l == depth - 1
        row = lambda a: a[l].reshape(1, -1)
        lam_init = 0.8 - 0.6 * math.exp(-0.3 * l)

        ffn_up = dict(tm=2048, tn=2 * V7X_MXU_DIM, ts=256, slab=V7X_LANES)
        act, wdb = _ffn_up(_norm(xt, row(ffn1_norm), tm=1024), ffn1_w_gate[l], ffn1_w_up[l],
                           ffn1_w_down[l], **ffn_up)
        x1, h = _ffn_down(act, wdb, xt, row(mix_norm), emit_y=True, tm=256)

        w_in = w_mix_in[l]
        zqkv = _qkv(h, w_in, tm=2048, tn=1024)
        cv = _conv_branch(h, w_in, conv_w[l], seq=s, tm=1024, tn=512, ts=256)
        slab = d // (b * DIFF_HEADS)
        ga_blk, gc_blk = 6, 7
        ya, (wa, wc, wga, wgc, wmo, wxq, wxo) = _diff_attn(
            zqkv, row(lambda_q1), row(lambda_k1), row(lambda_q2), row(lambda_k2), row(diff_subln),
            [(w_attn_out[l], slab, d, 0), (w_conv_out[l], slab, d, 0),
             (w_in, slab, d, ga_blk), (w_in, slab, d, gc_blk), (w_mix_out[l], slab, d, 0),
             (w_xq[l], slab, XATTN_WIDTH, 0), (w_xo[l], XATTN_WIDTH // (b * DIFF_HEADS), d, 0)],
            batch=b, seq=s, tq=256, lam_init=lam_init)
        m = _merge(ya, cv, h, wa, wc, wga, wgc, b_gates[l], tm=1024, tn=512, ts=256)

        kv = _mem_kv(memt, row(mem_norm), w_xkv[l], tm=512)
        x3, h3 = _mix_xattn(m, wmo, x1, row(xattn_norm), wxq, kv, wxo, row(ffn2_norm),
                            seq=s, tm=512, ts=256)

        g_next = final_norm.reshape(1, -1) if last else ffn1_norm[l + 1].reshape(1, -1)
        act, wdb = _ffn_up(h3, ffn2_w_gate[l], ffn2_w_up[l], ffn2_w_down[l], **ffn_up)
        res = _ffn_down(act, wdb, x3, g_next, emit_y=not last, tm=256)
        if last:
            out = res
        else:
            xt = res[0]
    return out.reshape(b, s, d)
```

```python
import functools
import math

import jax
import jax.numpy as jnp
from jax import lax
from jax.experimental import pallas as pl
from jax.experimental.pallas import tpu as pltpu

D_MODEL = 2048
MEM_LEN = 256
DIFF_HEADS = 8
DIFF_HEAD_DIM = 128
DIFF_PAIR = 2 * DIFF_HEAD_DIM
CONV_K = 3
XATTN_HEADS = 4
XATTN_HEAD_DIM = 128
XATTN_WIDTH = XATTN_HEADS * XATTN_HEAD_DIM
NORM_EPS = 1e-6
SUBLN_EPS = 1e-5
LOG2_E = math.log2(math.e)
SCORE_TILES_AHEAD = 2

V7X_LANES = 128
V7X_SUBLANES = 8
V7X_MXU_DIM = 256
V7X_VMEM_LIMIT_BYTES = 56 * 1024 * 1024

BF16 = jnp.bfloat16
F32 = jnp.float32


def _params(n_axes):
    return pltpu.CompilerParams(
        dimension_semantics=("arbitrary",) * n_axes,
        vmem_limit_bytes=V7X_VMEM_LIMIT_BYTES)


def _dot(a, b):
    return jnp.dot(a, b, preferred_element_type=F32)


def _dot_nt(a, b):
    return lax.dot_general(a, b, (((1,), (1,)), ((), ())), preferred_element_type=F32)


def _rms(x, g, eps):
    return x * lax.rsqrt(jnp.mean(x * x, axis=-1, keepdims=True) + eps) * g


def _sigmoid(x):
    return 1.0 / (1.0 + jnp.exp(-x))


def _cast_weights_once(step, pairs):
    @pl.when(step == 0)
    def _():
        for src, dst in pairs:
            dst[...] = src[...].astype(dst.dtype)


def _norm_kernel(x_ref, g_ref, o_ref):
    o_ref[...] = _rms(x_ref[...], g_ref[...], NORM_EPS).astype(o_ref.dtype)


def _norm(x, g, *, tm):
    t, d = x.shape
    row = pl.BlockSpec((tm, d), lambda i: (i, 0))
    return pl.pallas_call(
        _norm_kernel,
        grid=(t // tm,),
        in_specs=[row, pl.BlockSpec((1, d), lambda i: (0, 0))],
        out_specs=row,
        out_shape=jax.ShapeDtypeStruct((t, d), BF16),
        compiler_params=_params(1),
        name="ffn_norm",
    )(x, g)


def _ffn_up_kernel(h_ref, wg_ref, wu_ref, wd_ref, act_ref, wdb_ref, wgb_ref, wub_ref, *, ts):
    _cast_weights_once(pl.program_id(1), [(wg_ref, wgb_ref), (wu_ref, wub_ref)])
    wdb_ref[...] = wd_ref[...].astype(wdb_ref.dtype)
    wg, wu = wgb_ref[...], wub_ref[...]
    for r0 in range(0, h_ref.shape[0], ts):
        h = h_ref[r0:r0 + ts, :]
        gate = _dot(h, wg)
        act_ref[r0:r0 + ts, :] = (gate * _sigmoid(gate) * _dot(h, wu)).astype(act_ref.dtype)


def _ffn_up(h, wg, wu, wd, *, tm, tn, ts, slab):
    t, d = h.shape
    f = wg.shape[1]
    n_j, n_i = pl.cdiv(f, tn), t // tm
    n_slab = pl.cdiv(f, slab)
    assert n_slab <= n_j * n_i
    slab_idx = lambda j, i: (jnp.minimum(j * n_i + i, n_slab - 1), 0)
    col = pl.BlockSpec((d, tn), lambda j, i: (0, j))
    return pl.pallas_call(
        functools.partial(_ffn_up_kernel, ts=ts),
        grid=(n_j, n_i),
        in_specs=[pl.BlockSpec((tm, d), lambda j, i: (i, 0)), col, col,
                  pl.BlockSpec((slab, d), slab_idx)],
        out_specs=(pl.BlockSpec((tm, tn), lambda j, i: (i, j)),
                   pl.BlockSpec((slab, d), slab_idx)),
        out_shape=(jax.ShapeDtypeStruct((t, f), BF16), jax.ShapeDtypeStruct((f, d), BF16)),
        scratch_shapes=[pltpu.VMEM((d, tn), BF16)] * 2,
        compiler_params=_params(2),
        name="ffn_up",
    )(h, wg, wu, wd)


def _ffn_down_kernel(act_ref, wd_ref, x_ref, g2_ref, *out_refs, emit_y):
    y = x_ref[...] + 0.5 * _dot(act_ref[...], wd_ref[...])
    if emit_y:
        out_refs[0][...] = y
    out_refs[-1][...] = _rms(y, g2_ref[...], NORM_EPS).astype(out_refs[-1].dtype)


def _ffn_down(act, wd, x, g2, *, emit_y, tm):
    t, d = x.shape
    f = act.shape[1]
    row = pl.BlockSpec((tm, d), lambda i: (i, 0))
    in_specs = [pl.BlockSpec((tm, f), lambda i: (i, 0)),
                pl.BlockSpec((f, d), lambda i: (0, 0), pipeline_mode=pl.Buffered(1)),
                row, pl.BlockSpec((1, d), lambda i: (0, 0))]
    if emit_y:
        out_shape = (jax.ShapeDtypeStruct((t, d), F32), jax.ShapeDtypeStruct((t, d), BF16))
        out_specs = (row, row)
    else:
        out_shape = jax.ShapeDtypeStruct((t, d), F32)
        out_specs = row
    return pl.pallas_call(
        functools.partial(_ffn_down_kernel, emit_y=emit_y),
        grid=(t // tm,),
        in_specs=in_specs, out_specs=out_specs, out_shape=out_shape,
        compiler_params=_params(1),
        name="ffn_down_y" if emit_y else "ffn_down_final",
    )(act, wd, x, g2)


def _qkv_kernel(h_ref, w_ref, o_ref, wb_ref, *, q_blocks, scale):
    j = pl.program_id(0)
    _cast_weights_once(pl.program_id(1), [(w_ref, wb_ref)])
    acc = _dot(h_ref[...], wb_ref[...])
    acc = acc * jnp.where(j < q_blocks, scale, 1.0).astype(F32)
    o_ref[...] = acc.astype(o_ref.dtype)


def _qkv(h, w, *, tm, tn):
    t, d = h.shape
    n = 3 * d
    return pl.pallas_call(
        functools.partial(_qkv_kernel, q_blocks=d // tn, scale=DIFF_HEAD_DIM ** -0.5 * LOG2_E),
        grid=(n // tn, t // tm),
        in_specs=[pl.BlockSpec((tm, d), lambda j, i: (i, 0)),
                  pl.BlockSpec((d, tn), lambda j, i: (0, j))],
        out_specs=pl.BlockSpec((tm, tn), lambda j, i: (i, j)),
        out_shape=jax.ShapeDtypeStruct((t, n), BF16),
        scratch_shapes=[pltpu.VMEM((d, tn), BF16)],
        compiler_params=_params(2),
        name="qkv_proj",
    )(h, w)


def _conv_kernel(h_ref, wb_ref, wc_ref, wu_ref, cw_ref, o_ref, carry_ref, wbb_ref, wcb_ref, wub_ref,
                 *, tiles_per_seq, ts):
    i = pl.program_id(1)
    _cast_weights_once(i, [(wb_ref, wbb_ref), (wc_ref, wcb_ref), (wu_ref, wub_ref)])

    @pl.when(i % tiles_per_seq == 0)
    def _():
        carry_ref[...] = jnp.zeros_like(carry_ref)

    prev = carry_ref[...]
    row8 = lax.broadcasted_iota(jnp.int32, prev.shape, 0)
    cw = cw_ref[...]
    wb, wc, wu = wbb_ref[...], wcb_ref[...], wub_ref[...]

    for r0 in range(0, h_ref.shape[0], ts):
        h = h_ref[r0:r0 + ts, :]
        cu = _dot(h, wc) * _dot(h, wu)

        def shifted(s):
            rolled = pltpu.roll(cu, s, axis=0)
            head = jnp.where(row8 < s, pltpu.roll(prev, s, axis=0), rolled[:V7X_SUBLANES, :])
            return jnp.concatenate([head, rolled[V7X_SUBLANES:, :]], axis=0)

        y = cw[2:3, :] * cu + cw[1:2, :] * shifted(1) + cw[0:1, :] * shifted(2)
        o_ref[r0:r0 + ts, :] = (_dot(h, wb) * y).astype(o_ref.dtype)
        prev = cu[ts - V7X_SUBLANES:, :]
    carry_ref[...] = prev


def _conv_branch(h, w, conv_w, *, seq, tm, tn, ts):
    t, d = h.shape
    nb = d // tn
    wspec = lambda k: pl.BlockSpec((d, tn), lambda j, i: (0, k * nb + j))
    return pl.pallas_call(
        functools.partial(_conv_kernel, tiles_per_seq=seq // tm, ts=ts),
        grid=(nb, t // tm),
        in_specs=[pl.BlockSpec((tm, d), lambda j, i: (i, 0)),
                  wspec(3), wspec(4), wspec(5),
                  pl.BlockSpec((CONV_K, tn), lambda j, i: (0, j))],
        out_specs=pl.BlockSpec((tm, tn), lambda j, i: (i, j)),
        out_shape=jax.ShapeDtypeStruct((t, d), BF16),
        scratch_shapes=[pltpu.VMEM((V7X_SUBLANES, tn), F32)] + [pltpu.VMEM((d, tn), BF16)] * 3,
        compiler_params=_params(2),
        name="conv_branch",
    )(h, w, w, w, conv_w)


def _diff_attn_kernel(*refs, tq, nq, lam_init, n_round):
    q_ref, k_ref, v_ref, lq1_ref, lk1_ref, lq2_ref, lk2_ref, sg_ref = refs[:8]
    o_ref = refs[8 + n_round]
    for src_ref, dst_ref in zip(refs[8:8 + n_round], refs[9 + n_round:]):
        dst_ref[...] = src_ref[...].astype(dst_ref.dtype)

    hd = DIFF_HEAD_DIM
    lam = (jnp.exp(jnp.sum(lq1_ref[...] * lk1_ref[...], axis=-1, keepdims=True))
           - jnp.exp(jnp.sum(lq2_ref[...] * lk2_ref[...], axis=-1, keepdims=True))
           + lam_init)
    sg = sg_ref[...]
    keep = (lax.broadcasted_iota(jnp.int32, (tq, tq), 1)
            <= lax.broadcasted_iota(jnp.int32, (tq, tq), 0))

    def scores(qi):
        q = q_ref[qi * tq:(qi + 1) * tq, :]
        k = k_ref[0:(qi + 1) * tq, :]
        return _dot_nt(q[:, :hd], k[:, :hd]), _dot_nt(q[:, hd:], k[:, hd:])

    def softmax_terms(s, r0):
        s_diag = s[:, r0:]
        m = jnp.max(jnp.where(keep, s_diag, -jnp.inf), axis=-1, keepdims=True)
        if r0:
            m = jnp.maximum(m, jnp.max(s[:, :r0], axis=-1, keepdims=True))
        p = jnp.where(keep, jnp.exp2(s_diag - m), 0.0)
        if r0:
            p = jnp.concatenate([jnp.exp2(s[:, :r0] - m), p], axis=1)
        return p, jnp.sum(p, axis=-1, keepdims=True)

    order = list(reversed(range(nq)))
    ahead = SCORE_TILES_AHEAD
    queue = [scores(qi) for qi in order[:ahead]]
    for pos, qi in enumerate(order):
        r0 = qi * tq
        s1, s2 = queue.pop(0)
        if pos + ahead < nq:
            queue.append(scores(order[pos + ahead]))
        p1, l1 = softmax_terms(s1, r0)
        p2, l2 = softmax_terms(s2, r0)
        v = v_ref[0:r0 + tq, :]
        a = p1 - p2 * (lam * l1 / l2)
        o = _dot(a.astype(BF16), v) / l1
        o = _rms(o, sg, SUBLN_EPS) * (1.0 - lam_init)
        o_ref[r0:r0 + tq, :] = o.astype(o_ref.dtype)


def _diff_attn(zqkv, lq1, lk1, lq2, lk2, subln, round_jobs, *, batch, seq, tq, lam_init):
    t = zqkv.shape[0]
    w = DIFF_PAIR
    lvec = pl.BlockSpec((1, DIFF_HEAD_DIM), lambda b, h: (0, 0))
    round_in, round_out, round_shapes = [], [], []
    for a, rb, cw, cblk in round_jobs:
        n_slab = pl.cdiv(a.shape[0], rb)
        assert n_slab <= batch * DIFF_HEADS and rb % (2 * V7X_SUBLANES) == 0
        slab = lambda b, h, n_slab=n_slab: jnp.minimum(b * DIFF_HEADS + h, n_slab - 1)
        round_in.append(pl.BlockSpec((rb, cw), lambda b, h, slab=slab, cblk=cblk: (slab(b, h), cblk)))
        round_out.append(pl.BlockSpec((rb, cw), lambda b, h, slab=slab: (slab(b, h), 0)))
        round_shapes.append(jax.ShapeDtypeStruct((a.shape[0], cw), BF16))
    res = pl.pallas_call(
        functools.partial(_diff_attn_kernel, tq=tq, nq=seq // tq, lam_init=lam_init,
                          n_round=len(round_jobs)),
        grid=(batch, DIFF_HEADS),
        in_specs=[pl.BlockSpec((seq, w), lambda b, h: (b, h)),
                  pl.BlockSpec((seq, w), lambda b, h: (b, DIFF_HEADS + h)),
                  pl.BlockSpec((seq, w), lambda b, h: (b, 2 * DIFF_HEADS + h)),
                  lvec, lvec, lvec, lvec,
                  pl.BlockSpec((1, w), lambda b, h: (0, 0))] + round_in,
        out_specs=[pl.BlockSpec((seq, w), lambda b, h: (b, h))] + round_out,
        out_shape=[jax.ShapeDtypeStruct((t, DIFF_HEADS * w), BF16)] + round_shapes,
        compiler_params=_params(2),
        name="diff_attn",
    )(zqkv, zqkv, zqkv, lq1, lk1, lq2, lk2, subln, *[job[0] for job in round_jobs])
    return res[0], res[1:]


def _merge_kernel(ya_ref, cv_ref, h_ref, wa_ref, wc_ref, wga_ref, wgc_ref, b_ref, o_ref, *, ts):
    b = b_ref[...]
    wa, wc = wa_ref[...].astype(BF16), wc_ref[...].astype(BF16)
    wga, wgc = wga_ref[...].astype(BF16), wgc_ref[...].astype(BF16)
    for r0 in range(0, h_ref.shape[0], ts):
        h = h_ref[r0:r0 + ts, :]
        ga = _sigmoid(_dot(h, wga) + b[0:1, :])
        gc = _sigmoid(_dot(h, wgc) + b[1:2, :])
        m = ga * _dot(ya_ref[r0:r0 + ts, :], wa) + gc * _dot(cv_ref[r0:r0 + ts, :], wc)
        o_ref[r0:r0 + ts, :] = m.astype(o_ref.dtype)


def _merge(ya, cv, h, wa, wc, wga, wgc, b_gates, *, tm, tn, ts):
    t, d = h.shape
    row = pl.BlockSpec((tm, d), lambda i, j: (i, 0))
    col = pl.BlockSpec((d, tn), lambda i, j: (0, j))
    return pl.pallas_call(
        functools.partial(_merge_kernel, ts=ts),
        grid=(t // tm, d // tn),
        in_specs=[row, row, row, col, col, col, col,
                  pl.BlockSpec((2, tn), lambda i, j: (0, j))],
        out_specs=pl.BlockSpec((tm, tn), lambda i, j: (i, j)),
        out_shape=jax.ShapeDtypeStruct((t, d), BF16),
        compiler_params=_params(2),
        name="gated_merge",
    )(ya, cv, h, wa, wc, wga, wgc, b_gates)


def _mem_kv_kernel(m_ref, g_ref, w_ref, o_ref, wb_ref):
    _cast_weights_once(pl.program_id(0), [(w_ref, wb_ref)])
    hm = _rms(m_ref[...], g_ref[...], NORM_EPS).astype(BF16)
    o_ref[...] = _dot(hm, wb_ref[...]).astype(o_ref.dtype)


def _mem_kv(mem, g, w, *, tm):
    t, d = mem.shape
    n = w.shape[1]
    return pl.pallas_call(
        _mem_kv_kernel,
        grid=(t // tm,),
        in_specs=[pl.BlockSpec((tm, d), lambda i: (i, 0)),
                  pl.BlockSpec((1, d), lambda i: (0, 0)),
                  pl.BlockSpec((d, n), lambda i: (0, 0))],
        out_specs=pl.BlockSpec((tm, n), lambda i: (i, 0)),
        out_shape=jax.ShapeDtypeStruct((t, n), BF16),
        scratch_shapes=[pltpu.VMEM((d, n), BF16)],
        compiler_params=_params(1),
        name="mem_kv",
    )(mem, g, w)


def _mix_xattn_kernel(m_ref, wmo_ref, x_ref, g_ref, wq_ref, kv_ref, wo_ref, g2_ref, y_ref, n_ref,
                      *, ts):
    hd = XATTN_HEAD_DIM
    g, g2 = g_ref[...], g2_ref[...]
    kv = kv_ref[...]
    subs = range(0, x_ref.shape[0], ts)
    x2 = [x_ref[r0:r0 + ts, :] + _dot(m_ref[r0:r0 + ts, :], wmo_ref[...]) for r0 in subs]
    n = [_rms(a, g, NORM_EPS).astype(BF16) for a in x2]
    q = [(_dot(a, wq_ref[...]) * (hd ** -0.5)).astype(BF16) for a in n]
    outs = [[] for _ in subs]
    s = [[_dot_nt(a[:, h * hd:(h + 1) * hd], kv[:, h * hd:(h + 1) * hd]) for a in q]
         for h in range(XATTN_HEADS)]
    for h in range(XATTN_HEADS):
        v = kv[:, XATTN_WIDTH + h * hd:XATTN_WIDTH + (h + 1) * hd]
        p = [jnp.exp(a - jnp.max(a, axis=-1, keepdims=True)) for a in s[h]]
        for i, a in enumerate(p):
            l = jnp.sum(a, axis=-1, keepdims=True)
            outs[i].append((_dot(a.astype(BF16), v) / l).astype(BF16))
    o = [jnp.concatenate(a, axis=-1) for a in outs]
    y = [a + _dot(b, wo_ref[...]) for a, b in zip(x2, o)]
    for r0, a in zip(subs, y):
        y_ref[r0:r0 + ts, :] = a
        n_ref[r0:r0 + ts, :] = _rms(a, g2, NORM_EPS).astype(n_ref.dtype)


def _mix_xattn(m, wmo, x, g, wq, kv, wo, g2, *, seq, tm, ts):
    t, d = x.shape
    tiles_per_seq = seq // tm
    row = pl.BlockSpec((tm, d), lambda i: (i, 0))
    vec = pl.BlockSpec((1, d), lambda i: (0, 0))
    whole = lambda a: pl.BlockSpec(a.shape, lambda i: (0, 0), pipeline_mode=pl.Buffered(1))
    return pl.pallas_call(
        functools.partial(_mix_xattn_kernel, ts=ts),
        grid=(t // tm,),
        in_specs=[row, whole(wmo), row, vec, whole(wq),
                  pl.BlockSpec((MEM_LEN, 2 * XATTN_WIDTH), lambda i: (i // tiles_per_seq, 0)),
                  whole(wo), vec],
        out_specs=(row, row),
        out_shape=(jax.ShapeDtypeStruct((t, d), F32), jax.ShapeDtypeStruct((t, d), BF16)),
        compiler_params=_params(1),
        name="mix_xattn",
    )(m, wmo, x, g, wq, kv, wo, g2)


def kernel(x, mem, ffn1_norm, ffn1_w_gate, ffn1_w_up, ffn1_w_down, mix_norm, w_mix_in, b_gates,
           lambda_q1, lambda_k1, lambda_q2, lambda_k2, diff_subln, w_attn_out, conv_w, w_conv_out,
           w_mix_out, xattn_norm, mem_norm, w_xq, w_xkv, w_xo, ffn2_norm, ffn2_w_gate, ffn2_w_up,
           ffn2_w_down, final_norm):
    b, s, d = x.shape
    depth = ffn1_norm.shape[0]
    t = b * s

    xt = x.reshape(t, d)
    memt = mem.reshape(b * mem.shape[1], d)
    out = None
    for l in range(depth):
        last = l == depth - 1
        row = lambda a: a[l].reshape(1, -1)
        lam_init = 0.8 - 0.6 * math.exp(-0.3 * l)

        ffn_up = dict(tm=2048, tn=2 * V7X_MXU_DIM, ts=256, slab=V7X_LANES)
        act, wdb = _ffn_up(_norm(xt, row(ffn1_norm), tm=1024), ffn1_w_gate[l], ffn1_w_up[l],
                           ffn1_w_down[l], **ffn_up)
        x1, h = _ffn_down(act, wdb, xt, row(mix_norm), emit_y=True, tm=256)

        w_in = w_mix_in[l]
        zqkv = _qkv(h, w_in, tm=2048, tn=1024)
        cv = _conv_branch(h, w_in, conv_w[l], seq=s, tm=1024, tn=512, ts=256)
        slab = d // (b * DIFF_HEADS)
        ga_blk, gc_blk = 6, 7
        ya, (wa, wc, wga, wgc, wmo, wxq, wxo) = _diff_attn(
            zqkv, row(lambda_q1), row(lambda_k1), row(lambda_q2), row(lambda_k2), row(diff_subln),
            [(w_attn_out[l], slab, d, 0), (w_conv_out[l], slab, d, 0),
             (w_in, slab, d, ga_blk), (w_in, slab, d, gc_blk), (w_mix_out[l], slab, d, 0),
             (w_xq[l], slab, XATTN_WIDTH, 0), (w_xo[l], XATTN_WIDTH // (b * DIFF_HEADS), d, 0)],
            batch=b, seq=s, tq=256, lam_init=lam_init)
        m = _merge(ya, cv, h, wa, wc, wga, wgc, b_gates[l], tm=1024, tn=512, ts=256)

        kv = _mem_kv(memt, row(mem_norm), w_xkv[l], tm=512)
        x3, h3 = _mix_xattn(m, wmo, x1, row(xattn_norm), wxq, kv, wxo, row(ffn2_norm),
                            seq=s, tm=512, ts=256)

        g_next = final_norm.reshape(1, -1) if last else ffn1_norm[l + 1].reshape(1, -1)
        act, wdb = _ffn_up(h3, ffn2_w_gate[l], ffn2_w_up[l], ffn2_w_down[l], **ffn_up)
        res = _ffn_down(act, wdb, x3, g_next, emit_y=not last, tm=256)
        if last:
            out = res
        else:
            xt = res[0]
    return out.reshape(b, s, d)
```
